```python
import math, functools
import jax, jax.numpy as jnp
from jax import lax
import numpy as np

D_MODEL = 1024
BATCH = 4
SEQ = 4096
DEPTH = 4
DEC_BATCH = 8
DEC_SEQ = 64
PAST_LEN = 1024

CHUNK = 64
Q_BLOCK = 128
HEAD_DIM = 64
D_RWKV = D_MODEL // 2
N_RWKV_HEADS = D_RWKV // HEAD_DIM
LORA_DECAY = 64
LORA_ICLR = 64
N_DIFF_HEADS = D_MODEL // (4 * HEAD_DIM)
D_DIFF = N_DIFF_HEADS * 2 * HEAD_DIM
PLE_DIM = 256
NORM_EPS = 1e-6
GN_EPS = 64e-5
SUBLN_EPS = 1e-5
NEG_INF = -1e30
SHIFT_COLS = 3 * D_RWKV + LORA_DECAY + LORA_ICLR
RWKV_SIZES = (D_RWKV, D_RWKV, D_RWKV, LORA_DECAY, LORA_ICLR)
IN_SIZES = (SHIFT_COLS, D_RWKV, D_DIFF, D_DIFF, D_DIFF, D_DIFF, D_MODEL, D_MODEL)
IN_COLS = SHIFT_COLS + D_RWKV + 4 * D_DIFF + 2 * D_MODEL

kernel_name = 'hybrid_rwkv7_diffattn_stream_step'


def _split_points(sizes):
    return [int(s) for s in np.cumsum(sizes)[:-1]]


def rms_norm(x, g, eps=NORM_EPS):
    xf = x.astype(jnp.float32)
    y = xf * lax.rsqrt(jnp.mean(xf * xf, axis=-1, keepdims=True) + eps)
    return (y * g.astype(jnp.float32)).astype(x.dtype)


def head_group_norm(o, g, b):
    mu = jnp.mean(o, axis=-1, keepdims=True)
    var = jnp.mean(jnp.square(o - mu), axis=-1, keepdims=True)
    hn = o.shape[-2:]
    y = (o - mu) * lax.rsqrt(var + GN_EPS)
    y = y * g.astype(jnp.float32).reshape(hn) + b.astype(jnp.float32).reshape(hn)
    return y.reshape(o.shape[:-2] + (hn[0] * hn[1],))


def alibi_slopes():
    return jnp.asarray([2.0 ** (-8.0 * (h + 1) / N_DIFF_HEADS) for h in range(N_DIFF_HEADS)], dtype=jnp.float32)


def diff_attend(q, k, v, q_pos, k_pos, lam):
    s = jnp.einsum('bqhmd,bkhmd->bhmqk', q, k).astype(jnp.float32) * (HEAD_DIM ** -0.5)
    dist = jnp.abs(q_pos[:, None] - k_pos[None, :]).astype(jnp.float32)
    bias = -alibi_slopes()[:, None, None] * dist
    allowed = (k_pos[None, :] // CHUNK) <= (q_pos[:, None] // CHUNK)
    s = jnp.where(allowed, s + bias[:, None], NEG_INF)
    p = jax.nn.softmax(s, axis=-1)
    attn = p[:, :, 0] - lam * p[:, :, 1]
    return jnp.einsum('bhqk,bkhe->bqhe', attn.astype(v.dtype), v)


def prompt_attention(q, k, v, lam):
    b, t = q.shape[:2]
    nb = t // Q_BLOCK
    pos = jnp.arange(t, dtype=jnp.int32)
    qb = jnp.moveaxis(q.reshape((b, nb, Q_BLOCK) + q.shape[2:]), 1, 0)
    pb = pos.reshape(nb, Q_BLOCK)
    ob = lax.map(lambda blk: diff_attend(blk[0], k, v, blk[1], pos, lam), (qb, pb))
    return jnp.moveaxis(ob, 0, 1).reshape((b, t) + ob.shape[3:])


def sample_attention(q, k, v, lam, k_cache, v_cache):
    past, t = k_cache.shape[1], q.shape[1]
    k_all = jnp.concatenate([k_cache.astype(k.dtype), k], axis=1)
    v_all = jnp.concatenate([v_cache.astype(v.dtype), v], axis=1)
    k_pos = jnp.arange(past + t, dtype=jnp.int32)
    return diff_attend(q, k_all, v_all, k_pos[past:], k_pos, lam)


def wkv_scan(r, w, k, v, kk, a, s0):
    def step(S, inp):
        r_t, w_t, k_t, v_t, kk_t, a_t = inp
        S = (S * w_t[:, :, None, :]
             - jnp.einsum('bhvk,bhk->bhv', S, kk_t)[..., None] * (kk_t * a_t)[:, :, None, :]
             + v_t[..., :, None] * k_t[:, :, None, :])
        return S, jnp.einsum('bhvk,bhk->bhv', S, r_t)
    xs = tuple(jnp.moveaxis(u, 1, 0) for u in (r, w, k, v, kk, a))
    s_final, o = lax.scan(step, s0, xs)
    return jnp.moveaxis(o, 0, 1), s_final


def trunk_layer(x, p_emb, lp, layer_idx, shift_prev, wkv_prev, attend):
    b, t, _ = x.shape
    f32 = jnp.float32
    h = rms_norm(x, lp['norm_g'])
    z = h @ lp['w_in']
    z_shift, gate_r, q, k, v, gate_a, mg_r, mg_a = jnp.split(z, _split_points(IN_SIZES), axis=-1)

    z_prev = jnp.concatenate([shift_prev[:, None, :].astype(z.dtype), z_shift[:, :-1]], axis=1)
    zs = z_shift + (z_prev - z_shift) * lp['shift_mu']
    r_, k_, v_, wd, ad = jnp.split(zs, _split_points(RWKV_SIZES), axis=-1)
    w_log = -jax.nn.softplus(-(lp['decay_w0'] + jnp.tanh(wd) @ lp['decay_w2']).astype(f32)) - 0.5
    decay = jnp.exp(-jnp.exp(w_log))
    a = jax.nn.sigmoid((lp['iclr_a0'] + ad @ lp['iclr_a2']).astype(f32))

    def heads(u):
        return u.astype(f32).reshape(b, t, N_RWKV_HEADS, HEAD_DIM)

    kk = heads(k_ * lp['k_k'])
    kk = kk / jnp.maximum(jnp.sqrt(jnp.sum(kk * kk, axis=-1, keepdims=True)), 1e-12)
    kf = k_.astype(f32) * (1.0 + (a - 1.0) * lp['k_a'].astype(f32))
    rh, kh, vh, ah = heads(r_), heads(kf), heads(v_), heads(a)
    o_r, wkv_new = wkv_scan(rh, heads(decay), kh, vh, kk, ah, wkv_prev.astype(f32))
    bonus = jnp.sum(rh * kh * lp['r_k'].astype(f32), axis=-1, keepdims=True) * vh
    o_r = head_group_norm(o_r, lp['lnx_g'], lp['lnx_b']) + bonus.reshape(b, t, D_RWKV)
    o_r = o_r.astype(x.dtype) * jax.nn.silu(gate_r)

    qh = rms_norm(q.reshape(b, t, N_DIFF_HEADS, 2, HEAD_DIM), lp['q_norm_g'])
    kh_a = rms_norm(k.reshape(b, t, N_DIFF_HEADS, 2, HEAD_DIM), lp['k_norm_g'])
    vh_a = v.reshape(b, t, N_DIFF_HEADS, 2 * HEAD_DIM)
    lam_init = 0.8 - 0.6 * math.exp(-0.3 * layer_idx)
    lam = (jnp.exp(jnp.sum(lp['lambda_q1'].astype(f32) * lp['lambda_k1'].astype(f32)))
           - jnp.exp(jnp.sum(lp['lambda_q2'].astype(f32) * lp['lambda_k2'].astype(f32))) + lam_init)
    o_a = attend(qh, kh_a, vh_a, lam)
    o_a = rms_norm(o_a, lp['subln_g'], SUBLN_EPS) * (1.0 - lam_init)
    o_a = o_a.reshape(b, t, D_DIFF) * jax.nn.silu(gate_a)

    u = jax.nn.sigmoid(mg_r) * (o_r @ lp['w_br_r']) + jax.nn.sigmoid(mg_a) * (o_a @ lp['w_br_a'])
    x = x + u @ lp['w_out']
    e = p_emb @ lp['ple_w']
    x = x + e * jax.nn.sigmoid(rms_norm(x, lp['ple_norm_g']) @ lp['ple_gate_w'])
    return x, kh_a, vh_a, wkv_new.astype(x.dtype), z_shift[:, -1]


def setup_inputs(seed: int = 0) -> dict:
    key = jax.random.key(seed)
    ks = iter(jax.random.split(key, 48))
    f32 = jnp.float32

    def nrm(shape, scale):
        return jax.random.normal(next(ks), shape, f32) * scale

    L = DEPTH
    return {
        'x_prompt': nrm((BATCH, SEQ, D_MODEL), 1.0),
        'x_sample': nrm((DEC_BATCH, DEC_SEQ, D_MODEL), 1.0),
        'p_prompt': nrm((DEPTH, BATCH, SEQ, PLE_DIM), 1.0),
        'p_sample': nrm((DEPTH, DEC_BATCH, DEC_SEQ, PLE_DIM), 1.0),
        'cache_k': nrm((L, DEC_BATCH, PAST_LEN, N_DIFF_HEADS, 2, HEAD_DIM), 1.0),
        'cache_v': nrm((L, DEC_BATCH, PAST_LEN, N_DIFF_HEADS, 2 * HEAD_DIM), 1.0),
        'state_wkv': nrm((L, DEC_BATCH, N_RWKV_HEADS, HEAD_DIM, HEAD_DIM), 0.5),
        'state_shift': nrm((L, DEC_BATCH, SHIFT_COLS), 1.0),
        'norm_g': 1.0 + nrm((L, D_MODEL), 0.02),
        'w_in': nrm((L, D_MODEL, IN_COLS), D_MODEL ** -0.5),
        'shift_mu': jax.random.uniform(next(ks), (L, SHIFT_COLS), f32),
        'decay_w0': jax.random.uniform(next(ks), (L, D_RWKV), f32, -3.0, 0.0),
        'decay_w2': nrm((L, LORA_DECAY, D_RWKV), 0.5 * LORA_DECAY ** -0.5),
        'iclr_a0': nrm((L, D_RWKV), 0.5),
        'iclr_a2': nrm((L, LORA_ICLR, D_RWKV), 0.5 * LORA_ICLR ** -0.5),
        'k_k': 0.85 + nrm((L, D_RWKV), 0.05),
        'k_a': 1.0 + nrm((L, D_RWKV), 0.05),
        'r_k': nrm((L, N_RWKV_HEADS, HEAD_DIM), 0.1),
        'lnx_g': 1.0 + nrm((L, D_RWKV), 0.02),
        'lnx_b': nrm((L, D_RWKV), 0.02),
        'q_norm_g': 1.0 + nrm((L, HEAD_DIM), 0.02),
        'k_norm_g': 1.0 + nrm((L, HEAD_DIM), 0.02),
        'lambda_q1': nrm((L, HEAD_DIM), 0.1),
        'lambda_k1': nrm((L, HEAD_DIM), 0.1),
        'lambda_q2': nrm((L, HEAD_DIM), 0.1),
        'lambda_k2': nrm((L, HEAD_DIM), 0.1),
        'subln_g': 1.0 + nrm((L, 2 * HEAD_DIM), 0.02),
        'w_br_r': nrm((L, D_RWKV, D_MODEL), D_RWKV ** -0.5),
        'w_br_a': nrm((L, D_DIFF, D_MODEL), D_DIFF ** -0.5),
        'w_out': nrm((L, D_MODEL, D_MODEL), D_MODEL ** -0.5),
        'ple_w': nrm((L, PLE_DIM, D_MODEL), 0.5 * PLE_DIM ** -0.5),
        'ple_gate_w': nrm((L, D_MODEL, D_MODEL), D_MODEL ** -0.5),
        'ple_norm_g': 1.0 + nrm((L, D_MODEL), 0.02),
    }


def reference(x_prompt, x_sample, p_prompt, p_sample, cache_k, cache_v, state_wkv, state_shift,
              norm_g, w_in, shift_mu, decay_w0, decay_w2, iclr_a0, iclr_a2, k_k, k_a, r_k,
              lnx_g, lnx_b, q_norm_g, k_norm_g, lambda_q1, lambda_k1, lambda_q2, lambda_k2,
              subln_g, w_br_r, w_br_a, w_out, ple_w, ple_gate_w, ple_norm_g):
    y_p, y_s = x_prompt, x_sample
    kp_l, vp_l, wp_l, sp_l, ks_l, vs_l, ws_l, ss_l = [], [], [], [], [], [], [], []
    b_p = x_prompt.shape[0]
    for i in range(DEPTH):
        lp = dict(norm_g=norm_g[i], w_in=w_in[i], shift_mu=shift_mu[i], decay_w0=decay_w0[i],
                  decay_w2=decay_w2[i], iclr_a0=iclr_a0[i], iclr_a2=iclr_a2[i], k_k=k_k[i],
                  k_a=k_a[i], r_k=r_k[i], lnx_g=lnx_g[i], lnx_b=lnx_b[i], q_norm_g=q_norm_g[i],
                  k_norm_g=k_norm_g[i], lambda_q1=lambda_q1[i], lambda_k1=lambda_k1[i],
                  lambda_q2=lambda_q2[i], lambda_k2=lambda_k2[i], subln_g=subln_g[i],
                  w_br_r=w_br_r[i], w_br_a=w_br_a[i], w_out=w_out[i], ple_w=ple_w[i],
                  ple_gate_w=ple_gate_w[i], ple_norm_g=ple_norm_g[i])
        zero_shift = jnp.zeros((b_p, SHIFT_COLS), x_prompt.dtype)
        zero_wkv = jnp.zeros((b_p, N_RWKV_HEADS, HEAD_DIM, HEAD_DIM), jnp.float32)
        y_p, kp, vp, wp, sp = trunk_layer(y_p, p_prompt[i], lp, i, zero_shift, zero_wkv, prompt_attention)
        att_s = functools.partial(sample_attention, k_cache=cache_k[i], v_cache=cache_v[i])
        y_s, ks_, vs_, ws_, ss_ = trunk_layer(y_s, p_sample[i], lp, i, state_shift[i], state_wkv[i], att_s)
        kp_l.append(kp); vp_l.append(vp); wp_l.append(wp); sp_l.append(sp)
        ks_l.append(ks_); vs_l.append(vs_); ws_l.append(ws_); ss_l.append(ss_)
    return (y_p, y_s, jnp.stack(kp_l), jnp.stack(vp_l), jnp.stack(wp_l), jnp.stack(sp_l),
            jnp.stack(ks_l), jnp.stack(vs_l), jnp.stack(ws_l), jnp.stack(ss_l))
```

```python
import functools
import math

import jax
import jax.numpy as jnp
from jax import lax
from jax.experimental import pallas as pl
from jax.experimental.pallas import tpu as pltpu

F32 = jnp.float32
BF16 = jnp.bfloat16

D_MODEL = 1024
HEAD_DIM = 64
D_RWKV = 512
N_PAIRS = D_RWKV // (2 * HEAD_DIM)
PAIR = 2 * HEAD_DIM
LORA = 64
N_DIFF_HEADS = 4
D_DIFF = 512
PLE_DIM = 256
CHUNK = 64
NORM_EPS = 1e-6
GN_EPS = 64e-5
SUBLN_EPS = 1e-5
NEG_INF = -1e30
SHIFT_MAIN = 3 * D_RWKV
C_ZS, C_GR, C_Q, C_K, C_V, C_GA, C_MG, C_ZL, C_END = (
    0, 1536, 2048, 2560, 3072, 3584, 4096, 6144, 6272)

VMEM_LIMIT = 56 * 1024 * 1024


def _dot(a, b):
    return jnp.dot(a.astype(BF16), b.astype(BF16), preferred_element_type=F32)


def _dot_nt(a, b):
    return lax.dot_general(a.astype(BF16), b.astype(BF16), (((1,), (1,)), ((), ())),
                           preferred_element_type=F32)


def _dot_tn(a, b):
    return lax.dot_general(a.astype(BF16), b.astype(BF16), (((0,), (0,)), ((), ())),
                           preferred_element_type=F32)


def _dot_exact_rhs(a, b_exact, terms):
    acc = None
    rem = a
    for _ in range(terms):
        part = rem.astype(BF16)
        d = jnp.dot(part, b_exact, preferred_element_type=F32)
        acc = d if acc is None else acc + d
        rem = rem - part.astype(F32)
    return acc


def _dot_exact_lhs(a_exact, b, terms):
    acc = None
    rem = b
    for _ in range(terms):
        part = rem.astype(BF16)
        d = jnp.dot(a_exact, part, preferred_element_type=F32)
        acc = d if acc is None else acc + d
        rem = rem - part.astype(F32)
    return acc


def _sigmoid(x):
    return 1.0 / (1.0 + jnp.exp(-x))


def _silu(x):
    return x * _sigmoid(x)


def _const_spec(shape):
    nd = len(shape)
    return pl.BlockSpec(shape, lambda *_: (0,) * nd)


def _in_proj_kernel(x_ref, g_ref, w_ref, qg_ref, kg_ref, bd_ref,
                    zs_ref, zl_ref, gr_ref, ga_ref, mg_ref, k_ref, v_ref,
                    qb_ref, kb_ref, vb_ref):
    x = x_ref[...]
    ms = jnp.mean(x * x, axis=-1, keepdims=True)
    h = (x * lax.rsqrt(ms + NORM_EPS) * g_ref[...]).astype(BF16)

    def proj(lo, hi):
        return jnp.dot(h, w_ref[:, lo:hi], preferred_element_type=F32)

    zs_ref[...] = proj(C_ZS, C_GR)
    gr_ref[...] = proj(C_GR, C_Q)
    ga_ref[...] = proj(C_GA, C_MG)
    mg_ref[...] = proj(C_MG, C_ZL)
    zl_ref[...] = proj(C_ZL, C_END)
    v = proj(C_V, C_GA)
    v_ref[...] = v
    vb_ref[...] = v.astype(BF16)

    bd = bd_ref[...]

    def head_norm(t, gain):
        ss = _dot_exact_rhs(t * t, bd, 2)
        return t * lax.rsqrt(ss * (1.0 / HEAD_DIM) + NORM_EPS) * gain

    q = head_norm(proj(C_Q, C_K), qg_ref[...])
    qb_ref[...] = (q * (HEAD_DIM ** -0.5)).astype(BF16)
    k = head_norm(proj(C_K, C_V), kg_ref[...])
    k_ref[...] = k
    kb_ref[...] = k.astype(BF16)


def _in_proj(x, norm_g, w_perm, qg, kg, bd, tm):
    n = x.shape[0]
    grid = (n // tm,)
    row = lambda w: pl.BlockSpec((tm, w), lambda i: (i, 0))
    out_shapes = [
        jax.ShapeDtypeStruct((n, SHIFT_MAIN), F32),
        jax.ShapeDtypeStruct((n, 2 * LORA), F32),
        jax.ShapeDtypeStruct((n, D_RWKV), F32),
        jax.ShapeDtypeStruct((n, D_DIFF), F32),
        jax.ShapeDtypeStruct((n, 2 * D_MODEL), F32),
        jax.ShapeDtypeStruct((n, D_DIFF), F32),
        jax.ShapeDtypeStruct((n, D_DIFF), F32),
        jax.ShapeDtypeStruct((n, D_DIFF), BF16),
        jax.ShapeDtypeStruct((n, D_DIFF), BF16),
        jax.ShapeDtypeStruct((n, D_DIFF), BF16),
    ]
    out_specs = [row(s.shape[1]) for s in out_shapes]
    return pl.pallas_call(
        _in_proj_kernel,
        grid=grid,
        in_specs=[row(D_MODEL), _const_spec((1, D_MODEL)), _const_spec(w_perm.shape),
                  _const_spec((1, D_DIFF)), _const_spec((1, D_DIFF)), _const_spec((D_DIFF, D_DIFF))],
        out_specs=out_specs,
        out_shape=out_shapes,
        compiler_params=pltpu.CompilerParams(
            dimension_semantics=("parallel",), vmem_limit_bytes=VMEM_LIMIT),
        name="in_proj",
    )(x, norm_g, w_perm, qg, kg, bd)


def _rwkv_kernel(zs_ref, zl_ref, gr_ref, spm_ref, spl_ref, s0_ref,
                 mum_ref, mul_ref, wl_ref, w0_ref, a0_ref, kk_ref, ka_ref, rk_ref,
                 lg_ref, lb_ref, bd_ref,
                 o_ref, so_ref,
                 cm_scr, cl_scr, s_scr):
    t = pl.program_id(1)
    c = CHUNK

    @pl.when(t == 0)
    def _():
        cm_scr[...] = spm_ref[0]
        cl_scr[...] = spl_ref[0]
        s_scr[...] = s0_ref[0]

    row = lax.broadcasted_iota(jnp.int32, (c, 1), 0)

    def shifted(z_ref, carry, mu_ref):
        z = z_ref[0]
        zp = jnp.where(row == 0, carry[...], pltpu.roll(z, 1, axis=0))
        carry[...] = z[c - 1:c, :]
        return z + (zp - z) * mu_ref[...]

    zm = shifted(zs_ref, cm_scr, mum_ref)
    zl = shifted(zl_ref, cl_scr, mul_ref)
    r_ = zm[:, :D_RWKV]
    k_ = zm[:, D_RWKV:2 * D_RWKV]
    v_ = zm[:, 2 * D_RWKV:]

    lane = lax.broadcasted_iota(jnp.int32, (c, 2 * LORA), 1)
    lin = jnp.where(lane < LORA, jnp.tanh(zl), zl)
    lo = _dot(lin, wl_ref[...])
    wpre = w0_ref[...] + lo[:, :D_RWKV]
    apre = a0_ref[...] + lo[:, D_RWKV:]
    sp = jnp.maximum(-wpre, 0.0) + jnp.log(1.0 + jnp.exp(-jnp.abs(wpre)))
    lw = -jnp.exp(-sp - 0.5)
    a = _sigmoid(apre)

    bd = bd_ref[...]

    def head_sum(u):
        return _dot_exact_rhs(u, bd, 2)

    kkr = k_ * kk_ref[...]
    kk = kkr / jnp.maximum(jnp.sqrt(head_sum(kkr * kkr)), 1e-12)
    kf = k_ * (1.0 + (a - 1.0) * ka_ref[...])

    ri = lax.broadcasted_iota(jnp.int32, (c, c), 0)
    ci = lax.broadcasted_iota(jnp.int32, (c, c), 1)
    tri = jnp.where(ci <= ri, 1.0, 0.0).astype(BF16)
    cs = _dot_exact_lhs(tri, lw, 3)
    g = jnp.exp(cs)
    gprev = jnp.exp(cs - lw)
    ginv = jnp.exp(-cs)
    glast = g[c - 1:c, :]
    a_t = -kk * gprev
    b_t = kk * a * ginv
    k_t = kf * ginv
    r_t = r_ * g
    b_h = b_t * glast
    k_h = k_t * glast

    lane_p = lax.broadcasted_iota(jnp.int32, (c, PAIR), 1)
    first = lane_p < HEAD_DIM

    def stack(u):
        return jnp.concatenate([jnp.where(first, u, 0.0), jnp.where(first, 0.0, u)], axis=0)

    r2 = lax.broadcasted_iota(jnp.int32, (2 * c, 2 * c), 0)
    c2 = lax.broadcasted_iota(jnp.int32, (2 * c, 2 * c), 1)
    strict = c2 < r2
    eye = jnp.where(c2 == r2, 1.0, 0.0)
    r4 = lax.broadcasted_iota(jnp.int32, (2 * c, 4 * c), 0)
    c4 = lax.broadcasted_iota(jnp.int32, (2 * c, 4 * c), 1)
    incl = jnp.where(c4 >= 2 * c, c4 - 2 * c, c4) <= r4

    outs = []
    for p in range(N_PAIRS):
        sl = slice(p * PAIR, (p + 1) * PAIR)
        a_s, r_s = stack(a_t[:, sl]), stack(r_t[:, sl])
        b_s, k_s = stack(b_t[:, sl]), stack(k_t[:, sl])
        v_s = stack(v_[:, sl])
        bh_s, kh_s = stack(b_h[:, sl]), stack(k_h[:, sl])
        gmat = _dot_nt(jnp.concatenate([a_s, r_s], axis=0),
                       jnp.concatenate([b_s, k_s], axis=0))
        l_ab = jnp.where(strict, gmat[:2 * c, :2 * c], 0.0)
        a_ak = jnp.where(strict, gmat[:2 * c, 2 * c:], 0.0)
        a_r = jnp.where(incl, gmat[2 * c:, :], 0.0)
        inv = eye + l_ab
        lp = l_ab
        for _ in range(5):
            lp = _dot(lp, lp)
            inv = inv + _dot(inv, lp)
        w_s = _dot(inv, a_s)
        x_s = _dot(inv, _dot(a_ak, v_s))
        s = s_scr[p]
        u_s = _dot_nt(w_s, s) + x_s
        uv = jnp.concatenate([u_s, v_s], axis=0)
        o_s = _dot_nt(r_s, s) + _dot(a_r, uv)
        s_scr[p] = s * glast[:, sl] + _dot_tn(uv, jnp.concatenate([bh_s, kh_s], axis=0))
        outs.append(o_s[:c] + o_s[c:])
    o = jnp.concatenate(outs, axis=1)

    mu = head_sum(o) * (1.0 / HEAD_DIM)
    d = o - mu
    var = head_sum(d * d) * (1.0 / HEAD_DIM)
    bonus = head_sum(r_ * kf * rk_ref[...]) * v_
    y = d * lax.rsqrt(var + GN_EPS) * lg_ref[...] + lb_ref[...] + bonus
    o_ref[0] = (y * _silu(gr_ref[0])).astype(o_ref.dtype)

    @pl.when(t == pl.num_programs(1) - 1)
    def _():
        so_ref[0] = s_scr[...]


def _rwkv(zs, zl, gr, spm, spl, s0, prm, bd):
    b, t, _ = zs.shape
    c = CHUNK
    grid = (b, t // c)
    seq = lambda w: pl.BlockSpec((1, c, w), lambda i, j: (i, j, 0))
    per_b = lambda shp: pl.BlockSpec((1,) + shp, lambda i, j: (i,) + (0,) * len(shp))
    params = [prm["mu_main"], prm["mu_lora"], prm["w_lora"], prm["w0"], prm["a0"], prm["k_k"],
              prm["k_a"], prm["r_k"], prm["lnx_g"], prm["lnx_b"], bd]
    return pl.pallas_call(
        _rwkv_kernel,
        grid=grid,
        in_specs=[seq(SHIFT_MAIN), seq(2 * LORA), seq(D_RWKV),
                  per_b((1, SHIFT_MAIN)), per_b((1, 2 * LORA)), per_b((N_PAIRS, PAIR, PAIR))]
                 + [_const_spec(p.shape) for p in params],
        out_specs=[seq(D_RWKV), per_b((N_PAIRS, PAIR, PAIR))],
        out_shape=[jax.ShapeDtypeStruct((b, t, D_RWKV), F32),
                   jax.ShapeDtypeStruct((b, N_PAIRS, PAIR, PAIR), F32)],
        scratch_shapes=[pltpu.VMEM((1, SHIFT_MAIN), F32), pltpu.VMEM((1, 2 * LORA), F32),
                        pltpu.VMEM((N_PAIRS, PAIR, PAIR), F32)],
        compiler_params=pltpu.CompilerParams(
            dimension_semantics=("parallel", "arbitrary"), vmem_limit_bytes=VMEM_LIMIT),
        name="rwkv",
    )(zs, zl, gr, spm, spl, s0, *params)


def _attn_kernel(scal_ref, q_ref, k_ref, v_ref, ga_ref, sg_ref, o_ref,
                 qs_scr, rel_scr, m_scr, l_scr, acc_scr, *, bq, bk, q_off):
    h = pl.program_id(1)
    qi = pl.program_id(2)
    ki = pl.program_id(3)
    nk = pl.num_programs(3)
    slope = scal_ref[h]
    q0 = q_off + qi * bq
    k0 = ki * bk
    k_last = jnp.minimum(nk - 1, (q0 + bq - 1) // bk)

    @pl.when(ki == 0)
    def _():
        q = q_ref[0]
        first = lax.broadcasted_iota(jnp.int32, (bq, PAIR), 1) < HEAD_DIM
        zero = jnp.zeros_like(q)
        qs_scr[...] = jnp.concatenate([jnp.where(first, q, zero), jnp.where(first, zero, q)], axis=0)
        ri = lax.broadcasted_iota(jnp.int32, (2 * bq, bk), 0)
        ri = jnp.where(ri >= bq, ri - bq, ri)
        ci = lax.broadcasted_iota(jnp.int32, (2 * bq, bk), 1)
        rel_scr[...] = (ci - ri).astype(F32) * slope
        m_scr[...] = jnp.full(m_scr.shape, NEG_INF, F32)
        l_scr[...] = jnp.zeros(l_scr.shape, F32)
        acc_scr[...] = jnp.zeros(acc_scr.shape, F32)

    def update(s_eff, off):
        m_old = m_scr[...]
        m_new = jnp.maximum(m_old, jnp.max(s_eff, axis=1, keepdims=True) + off)
        p = jnp.exp(s_eff - (m_new - off))
        alpha = jnp.exp(m_old - m_new)
        l_scr[...] = alpha * l_scr[...] + jnp.sum(p, axis=1, keepdims=True)
        acc_scr[...] = alpha * acc_scr[...] + jnp.dot(
            p.astype(BF16), v_ref[0], preferred_element_type=F32)
        m_scr[...] = m_new

    def scores():
        return lax.dot_general(qs_scr[...], k_ref[0], (((1,), (1,)), ((), ())),
                               preferred_element_type=F32)

    past = k0 + bk - 1 <= q0

    @pl.when(jnp.logical_and(ki <= k_last, past))
    def _():
        update(scores() + rel_scr[...], -slope * (q0 - k0).astype(F32))

    @pl.when(jnp.logical_and(ki <= k_last, jnp.logical_not(past)))
    def _():
        ri = lax.broadcasted_iota(jnp.int32, (2 * bq, bk), 0)
        qpos = q0 + jnp.where(ri >= bq, ri - bq, ri)
        kpos = k0 + lax.broadcasted_iota(jnp.int32, (2 * bq, bk), 1)
        s = scores() - slope * jnp.abs(qpos - kpos).astype(F32)
        allowed = (kpos // CHUNK) <= (qpos // CHUNK)
        update(jnp.where(allowed, s, NEG_INF), 0.0)

    @pl.when(ki == nk - 1)
    def _():
        lam = scal_ref[N_DIFF_HEADS]
        post = scal_ref[N_DIFF_HEADS + 1]
        on = acc_scr[...] / l_scr[...]
        o = on[:bq] - lam * on[bq:]
        o = o * lax.rsqrt(jnp.mean(o * o, axis=-1, keepdims=True) + SUBLN_EPS) * sg_ref[...] * post
        o_ref[0] = (o * _silu(ga_ref[0])).astype(o_ref.dtype)


def _attn(scal, qb, kb, vb, ga, subln_g, bq, bk, q_off):
    b, tq, _ = qb.shape
    tk = kb.shape[1]
    nq, nk = tq // bq, tk // bk
    grid = (b, N_DIFF_HEADS, nq, nk)

    def kv_map(i, h, qi, ki):
        return (i, jnp.minimum(ki, (q_off + qi * bq + bq - 1) // bk), h)

    q_spec = pl.BlockSpec((1, bq, PAIR), lambda i, h, qi, ki: (i, qi, h))
    kv_spec = pl.BlockSpec((1, bk, PAIR), kv_map)
    kern = functools.partial(_attn_kernel, bq=bq, bk=bk, q_off=q_off)
    return pl.pallas_call(
        kern,
        grid=grid,
        in_specs=[pl.BlockSpec(memory_space=pltpu.SMEM), q_spec, kv_spec, kv_spec, q_spec,
                  _const_spec((1, PAIR))],
        out_specs=q_spec,
        out_shape=jax.ShapeDtypeStruct((b, tq, D_DIFF), F32),
        scratch_shapes=[pltpu.VMEM((2 * bq, PAIR), BF16), pltpu.VMEM((2 * bq, bk), F32),
                        pltpu.VMEM((2 * bq, 1), F32), pltpu.VMEM((2 * bq, 1), F32),
                        pltpu.VMEM((2 * bq, PAIR), F32)],
        compiler_params=pltpu.CompilerParams(
            dimension_semantics=("parallel", "parallel", "parallel", "arbitrary"),
            vmem_limit_bytes=VMEM_LIMIT),
        name="attn",
    )(scal, qb, kb, vb, ga, subln_g)


def _merge_kernel(x_ref, or_ref, oa_ref, mg_ref, p_ref, wr_ref, wa_ref, wo_ref, pw_ref, pg_ref,
                  png_ref, y_ref):
    mg = mg_ref[...]
    u = (_sigmoid(mg[:, :D_MODEL]) * _dot(or_ref[...], wr_ref[...])
         + _sigmoid(mg[:, D_MODEL:]) * _dot(oa_ref[...], wa_ref[...]))
    x = x_ref[...] + _dot(u, wo_ref[...])
    e = _dot(p_ref[...], pw_ref[...])
    ms = jnp.mean(x * x, axis=-1, keepdims=True)
    hn = x * lax.rsqrt(ms + NORM_EPS) * png_ref[...]
    y_ref[...] = x + e * _sigmoid(_dot(hn, pg_ref[...]))


def _merge(x, o_r, o_a, mg, p, prm, tm):
    n = x.shape[0]
    row = lambda w: pl.BlockSpec((tm, w), lambda i: (i, 0))
    ws = [prm["w_br_r"], prm["w_br_a"], prm["w_out"], prm["ple_w"], prm["ple_gate_w"],
          prm["ple_norm_g"]]
    return pl.pallas_call(
        _merge_kernel,
        grid=(n // tm,),
        in_specs=[row(D_MODEL), row(D_RWKV), row(D_DIFF), row(2 * D_MODEL), row(PLE_DIM)]
                 + [_const_spec(w.shape) for w in ws],
        out_specs=row(D_MODEL),
        out_shape=jax.ShapeDtypeStruct((n, D_MODEL), F32),
        compiler_params=pltpu.CompilerParams(
            dimension_semantics=("parallel",), vmem_limit_bytes=VMEM_LIMIT),
        name="merge",
    )(x, o_r, o_a, mg, p, *ws)


def _layer_params(i, norm_g, w_in, shift_mu, decay_w0, decay_w2, iclr_a0, iclr_a2, k_k, k_a, r_k,
                  lnx_g, lnx_b, q_norm_g, k_norm_g, lambda_q1, lambda_k1, lambda_q2, lambda_k2,
                  subln_g, w_br_r, w_br_a, w_out, ple_w, ple_gate_w, ple_norm_g):
    w = w_in[i]
    s = SHIFT_MAIN
    e = s + 2 * LORA
    w_perm = jnp.concatenate([w[:, :s], w[:, e:], w[:, s:e]], axis=1).astype(BF16)
    zeros = jnp.zeros((LORA, D_RWKV), F32)
    w_lora = jnp.concatenate([jnp.concatenate([decay_w2[i], zeros], axis=1),
                              jnp.concatenate([zeros, iclr_a2[i]], axis=1)], axis=0).astype(BF16)
    lam_init = 0.8 - 0.6 * math.exp(-0.3 * i)
    lam = (jnp.exp(jnp.sum(lambda_q1[i] * lambda_k1[i]))
           - jnp.exp(jnp.sum(lambda_q2[i] * lambda_k2[i])) + lam_init)
    slopes = jnp.asarray([2.0 ** (-8.0 * (h + 1) / N_DIFF_HEADS) for h in range(N_DIFF_HEADS)], F32)
    scal = jnp.concatenate([slopes, jnp.stack([lam, jnp.asarray(1.0 - lam_init, F32)])]).astype(F32)
    r2 = lambda u: u.reshape(1, -1)
    return dict(
        norm_g=r2(norm_g[i]), w_perm=w_perm,
        qg=r2(jnp.tile(q_norm_g[i], 2 * N_DIFF_HEADS)), kg=r2(jnp.tile(k_norm_g[i], 2 * N_DIFF_HEADS)),
        mu_main=r2(shift_mu[i, :s]), mu_lora=r2(shift_mu[i, s:]), w_lora=w_lora,
        w0=r2(decay_w0[i]), a0=r2(iclr_a0[i]), k_k=r2(k_k[i]), k_a=r2(k_a[i]), r_k=r2(r_k[i]),
        lnx_g=r2(lnx_g[i]), lnx_b=r2(lnx_b[i]), scal=scal, subln_g=r2(subln_g[i]),
        w_br_r=w_br_r[i].astype(BF16), w_br_a=w_br_a[i].astype(BF16), w_out=w_out[i].astype(BF16),
        ple_w=ple_w[i].astype(BF16), ple_gate_w=ple_gate_w[i].astype(BF16),
        ple_norm_g=r2(ple_norm_g[i]))


def _pair_state(s):
    b = s.shape[0]
    s = s.reshape(b, N_PAIRS, 2, HEAD_DIM, HEAD_DIM)
    z = jnp.zeros_like(s[:, :, 0])
    top = jnp.concatenate([s[:, :, 0], z], axis=-1)
    bot = jnp.concatenate([z, s[:, :, 1]], axis=-1)
    return jnp.concatenate([top, bot], axis=-2)


def _unpair_state(sp):
    b = sp.shape[0]
    h0 = sp[:, :, :HEAD_DIM, :HEAD_DIM]
    h1 = sp[:, :, HEAD_DIM:, HEAD_DIM:]
    return jnp.stack([h0, h1], axis=2).reshape(b, 2 * N_PAIRS, HEAD_DIM, HEAD_DIM)


def _group_layer(x, p_emb, prm, bd, shift_prev, wkv_prev, cache, tm, bq):
    b, t, _ = x.shape
    n = b * t
    zs, zl, gr, ga, mg, k, v, qb, kb, vb = _in_proj(
        x.reshape(n, D_MODEL), prm["norm_g"], prm["w_perm"], prm["qg"], prm["kg"], bd, tm)
    seq = lambda u: u.reshape(b, t, u.shape[-1])
    zs, zl = seq(zs), seq(zl)
    o_r, s_new = _rwkv(zs, zl, seq(gr), shift_prev[:, None, :SHIFT_MAIN],
                       shift_prev[:, None, SHIFT_MAIN:], _pair_state(wkv_prev), prm, bd)
    kb, vb = seq(kb), seq(vb)
    if cache is None:
        k_all, v_all, q_off, bk = kb, vb, 0, 2 * bq
    else:
        ck, cv = cache
        past = ck.shape[1]
        k_all = jnp.concatenate([ck.reshape(b, past, D_DIFF).astype(BF16), kb], axis=1)
        v_all = jnp.concatenate([cv.reshape(b, past, D_DIFF).astype(BF16), vb], axis=1)
        q_off, bk = past, past + t
    o_a = _attn(prm["scal"], seq(qb), k_all, v_all, seq(ga), prm["subln_g"], bq, bk, q_off)
    y = _merge(x.reshape(n, D_MODEL), o_r.reshape(n, D_RWKV), o_a.reshape(n, D_DIFF), mg,
               p_emb.reshape(n, PLE_DIM), prm, tm)
    shift_new = jnp.concatenate([zs[:, -1], zl[:, -1]], axis=-1)
    return (y.reshape(b, t, D_MODEL), k.reshape(b, t, N_DIFF_HEADS, 2, HEAD_DIM),
            v.reshape(b, t, N_DIFF_HEADS, 2 * HEAD_DIM), _unpair_state(s_new), shift_new)


def kernel(x_prompt, x_sample, p_prompt, p_sample, cache_k, cache_v, state_wkv, state_shift, norm_g, w_in, shift_mu, decay_w0, decay_w2, iclr_a0, iclr_a2, k_k, k_a, r_k, lnx_g, lnx_b, q_norm_g, k_norm_g, lambda_q1, lambda_k1, lambda_q2, lambda_k2, subln_g, w_br_r, w_br_a, w_out, ple_w, ple_gate_w, ple_norm_g):
    depth = w_in.shape[0]
    b_p = x_prompt.shape[0]
    hid = jnp.arange(D_DIFF) // HEAD_DIM
    bd = (hid[:, None] == hid[None, :]).astype(BF16)
    zero_shift = jnp.zeros((b_p, SHIFT_MAIN + 2 * LORA), F32)
    zero_wkv = jnp.zeros((b_p, 2 * N_PAIRS, HEAD_DIM, HEAD_DIM), F32)
    y_p, y_s = x_prompt, x_sample
    outs_p, outs_s = [], []
    for i in range(depth):
        prm = _layer_params(i, norm_g, w_in, shift_mu, decay_w0, decay_w2, iclr_a0, iclr_a2, k_k,
                            k_a, r_k, lnx_g, lnx_b, q_norm_g, k_norm_g, lambda_q1, lambda_k1,
                            lambda_q2, lambda_k2, subln_g, w_br_r, w_br_a, w_out, ple_w,
                            ple_gate_w, ple_norm_g)
        y_p, *rest_p = _group_layer(y_p, p_prompt[i], prm, bd, zero_shift, zero_wkv, None,
                                    tm=256, bq=256)
        y_s, *rest_s = _group_layer(y_s, p_sample[i], prm, bd, state_shift[i], state_wkv[i],
                                    (cache_k[i], cache_v[i]), tm=256, bq=x_sample.shape[1])
        outs_p.append(rest_p)
        outs_s.append(rest_s)
    stack = lambda outs, j: jnp.stack([o[j] for o in outs])
    return (y_p, y_s, stack(outs_p, 0), stack(outs_p, 1), stack(outs_p, 2), stack(outs_p, 3),
            stack(outs_s, 0), stack(outs_s, 1), stack(outs_s, 2), stack(outs_s, 3))
```

```python
import functools
import math

import jax
import jax.numpy as jnp
from jax import lax
from jax.experimental import pallas as pl
from jax.experimental.pallas import tpu as pltpu

F32 = jnp.float32
BF16 = jnp.bfloat16

D_MODEL = 1024
HEAD_DIM = 64
D_RWKV = 512
N_PAIRS = D_RWKV // (2 * HEAD_DIM)
PAIR = 2 * HEAD_DIM
LORA = 64
N_DIFF_HEADS = 4
D_DIFF = 512
PLE_DIM = 256
CHUNK = 64
NORM_EPS = 1e-6
GN_EPS = 64e-5
SUBLN_EPS = 1e-5
NEG_INF = -1e30
SHIFT_MAIN = 3 * D_RWKV
C_ZS, C_GR, C_Q, C_K, C_V, C_GA, C_MG, C_ZL, C_END = (
    0, 1536, 2048, 2560, 3072, 3584, 4096, 6144, 6272)

ATTN_KEY_TILE = 256
HEADS_PER_STEP = 2
REL, DIAG, NO_KEYS = 0, 1, 2
LOG2E = 1.4426950408889634
VMEM_LIMIT = 56 * 1024 * 1024


def _dot(a, b):
    return jnp.dot(a.astype(BF16), b.astype(BF16), preferred_element_type=F32)


def _dot_nt(a, b):
    return lax.dot_general(a.astype(BF16), b.astype(BF16), (((1,), (1,)), ((), ())),
                           preferred_element_type=F32)


def _dot_tn(a, b):
    return lax.dot_general(a.astype(BF16), b.astype(BF16), (((0,), (0,)), ((), ())),
                           preferred_element_type=F32)


def _dot_exact_rhs(a, b_exact, terms):
    acc = None
    rem = a
    for _ in range(terms):
        part = rem.astype(BF16)
        d = jnp.dot(part, b_exact, preferred_element_type=F32)
        acc = d if acc is None else acc + d
        rem = rem - part.astype(F32)
    return acc


def _dot_exact_lhs(a_exact, b, terms):
    acc = None
    rem = b
    for _ in range(terms):
        part = rem.astype(BF16)
        d = jnp.dot(a_exact, part, preferred_element_type=F32)
        acc = d if acc is None else acc + d
        rem = rem - part.astype(F32)
    return acc


def _sigmoid(x):
    return 1.0 / (1.0 + jnp.exp(-x))


def _silu(x):
    return x * _sigmoid(x)


def _const_spec(shape):
    nd = len(shape)
    return pl.BlockSpec(shape, lambda *_: (0,) * nd)


def _in_proj_kernel(x_ref, g_ref, w_ref, qg_ref, kg_ref, bd_ref,
                    zs_ref, zl_ref, gr_ref, ga_ref, mg_ref, k_ref, v_ref,
                    qb_ref, kb_ref, vb_ref):
    x = x_ref[...]
    ms = jnp.mean(x * x, axis=-1, keepdims=True)
    h = (x * lax.rsqrt(ms + NORM_EPS) * g_ref[...]).astype(BF16)

    def proj(lo, hi):
        return jnp.dot(h, w_ref[:, lo:hi], preferred_element_type=F32)

    zs_ref[...] = proj(C_ZS, C_GR)
    gr_ref[...] = proj(C_GR, C_Q)
    ga_ref[...] = proj(C_GA, C_MG)
    mg_ref[...] = proj(C_MG, C_ZL)
    zl_ref[...] = proj(C_ZL, C_END)
    v = proj(C_V, C_GA)
    v_ref[...] = v
    vb_ref[...] = v.astype(BF16)

    bd = bd_ref[...]

    def head_norm(t, gain):
        ss = _dot_exact_rhs(t * t, bd, 2)
        return t * lax.rsqrt(ss * (1.0 / HEAD_DIM) + NORM_EPS) * gain

    q = head_norm(proj(C_Q, C_K), qg_ref[...])
    qb_ref[...] = (q * (LOG2E * HEAD_DIM ** -0.5)).astype(BF16)
    k = head_norm(proj(C_K, C_V), kg_ref[...])
    k_ref[...] = k
    kb_ref[...] = k.astype(BF16)


def _in_proj(x, norm_g, w_perm, qg, kg, bd, tm):
    n = x.shape[0]
    grid = (n // tm,)
    row = lambda w: pl.BlockSpec((tm, w), lambda i: (i, 0))
    out_shapes = [
        jax.ShapeDtypeStruct((n, SHIFT_MAIN), F32),
        jax.ShapeDtypeStruct((n, 2 * LORA), F32),
        jax.ShapeDtypeStruct((n, D_RWKV), F32),
        jax.ShapeDtypeStruct((n, D_DIFF), F32),
        jax.ShapeDtypeStruct((n, 2 * D_MODEL), F32),
        jax.ShapeDtypeStruct((n, D_DIFF), F32),
        jax.ShapeDtypeStruct((n, D_DIFF), F32),
        jax.ShapeDtypeStruct((n, D_DIFF), BF16),
        jax.ShapeDtypeStruct((n, D_DIFF), BF16),
        jax.ShapeDtypeStruct((n, D_DIFF), BF16),
    ]
    out_specs = [row(s.shape[1]) for s in out_shapes]
    return pl.pallas_call(
        _in_proj_kernel,
        grid=grid,
        in_specs=[row(D_MODEL), _const_spec((1, D_MODEL)), _const_spec(w_perm.shape),
                  _const_spec((1, D_DIFF)), _const_spec((1, D_DIFF)), _const_spec((D_DIFF, D_DIFF))],
        out_specs=out_specs,
        out_shape=out_shapes,
        compiler_params=pltpu.CompilerParams(
            dimension_semantics=("parallel",), vmem_limit_bytes=VMEM_LIMIT),
        name="in_proj",
    )(x, norm_g, w_perm, qg, kg, bd)


def _rwkv_kernel(zs_ref, zl_ref, gr_ref, spm_ref, spl_ref, s0_ref,
                 mum_ref, mul_ref, wl_ref, w0_ref, a0_ref, kk_ref, ka_ref, rk_ref,
                 lg_ref, lb_ref, bd_ref,
                 o_ref, so_ref,
                 cm_scr, cl_scr, s_scr, *, tb):
    t = pl.program_id(1)
    c = CHUNK
    n_chunks = tb // c

    @pl.when(t == 0)
    def _():
        cm_scr[...] = spm_ref[0]
        cl_scr[...] = spl_ref[0]
        s_scr[...] = s0_ref[0]

    row = lax.broadcasted_iota(jnp.int32, (tb, 1), 0)

    def shifted(z_ref, carry, mu_ref):
        z = z_ref[0]
        zp = jnp.where(row == 0, carry[...], pltpu.roll(z, 1, axis=0))
        carry[...] = z[tb - 1:tb, :]
        return z + (zp - z) * mu_ref[...]

    zm = shifted(zs_ref, cm_scr, mum_ref)
    zl = shifted(zl_ref, cl_scr, mul_ref)
    r_ = zm[:, :D_RWKV]
    k_ = zm[:, D_RWKV:2 * D_RWKV]
    v_ = zm[:, 2 * D_RWKV:]

    lane = lax.broadcasted_iota(jnp.int32, (tb, 2 * LORA), 1)
    lin = jnp.where(lane < LORA, jnp.tanh(zl), zl)
    lo = _dot(lin, wl_ref[...])
    wpre = w0_ref[...] + lo[:, :D_RWKV]
    apre = a0_ref[...] + lo[:, D_RWKV:]
    sp = jnp.maximum(-wpre, 0.0) + jnp.log(1.0 + jnp.exp(-jnp.abs(wpre)))
    lw = -jnp.exp(-sp - 0.5)
    a = _sigmoid(apre)

    bd = bd_ref[...]

    def head_sum(u):
        return _dot_exact_rhs(u, bd, 2)

    kkr = k_ * kk_ref[...]
    kk = kkr / jnp.maximum(jnp.sqrt(head_sum(kkr * kkr)), 1e-12)
    kf = k_ * (1.0 + (a - 1.0) * ka_ref[...])

    ri = lax.broadcasted_iota(jnp.int32, (tb, tb), 0)
    ci = lax.broadcasted_iota(jnp.int32, (tb, tb), 1)
    same = (ri >> 6) == (ci >> 6)
    tri = jnp.where(jnp.logical_and(same, ci <= ri), 1.0, 0.0).astype(BF16)
    cs = _dot_exact_lhs(tri, lw, 3)
    if n_chunks == 1:
        ctot = cs[c - 1:c, :]
    else:
        ctot = _dot_exact_lhs(jnp.where(same, 1.0, 0.0).astype(BF16), lw, 3)
    g = jnp.exp(cs)
    ginv = jnp.exp(-cs)
    gend = jnp.exp(ctot - cs)
    ka = kk * a
    a_t = -kk * jnp.exp(cs - lw)
    b_t = ka * ginv
    k_t = kf * ginv
    r_t = r_ * g
    b_h = ka * gend
    k_h = kf * gend

    lane_p = lax.broadcasted_iota(jnp.int32, (c, PAIR), 1)
    first = lane_p < HEAD_DIM

    def stack(u):
        return jnp.concatenate([jnp.where(first, u, 0.0), jnp.where(first, 0.0, u)], axis=0)

    r2 = lax.broadcasted_iota(jnp.int32, (2 * c, 2 * c), 0)
    c2 = lax.broadcasted_iota(jnp.int32, (2 * c, 2 * c), 1)
    strict = c2 < r2
    eye = jnp.where(c2 == r2, 1.0, 0.0)
    r4 = lax.broadcasted_iota(jnp.int32, (2 * c, 4 * c), 0)
    c4 = lax.broadcasted_iota(jnp.int32, (2 * c, 4 * c), 1)
    incl = jnp.where(c4 >= 2 * c, c4 - 2 * c, c4) <= r4

    blocks = [(ch, p) for ch in range(n_chunks) for p in range(N_PAIRS)]

    def gather(u):
        return [stack(u[ch * c:(ch + 1) * c, p * PAIR:(p + 1) * PAIR]) for ch, p in blocks]

    a_s, r_s, b_s, k_s, v_s = gather(a_t), gather(r_t), gather(b_t), gather(k_t), gather(v_)
    bkh = [jnp.concatenate([x, y], axis=0).astype(BF16) for x, y in zip(gather(b_h), gather(k_h))]
    gmat = [_dot_nt(jnp.concatenate([x, y], axis=0), jnp.concatenate([z, w], axis=0))
            for x, y, z, w in zip(a_s, r_s, b_s, k_s)]
    a_ak = [jnp.where(strict, gm[:2 * c, 2 * c:], 0.0) for gm in gmat]
    a_r = [jnp.where(incl, gm[2 * c:, :], 0.0).astype(BF16) for gm in gmat]
    lp = [jnp.where(strict, gm[:2 * c, :2 * c], 0.0) for gm in gmat]
    inv = [eye + x for x in lp]
    for _ in range(5):
        lp = [_dot(x, x) for x in lp]
        inv = [y + _dot(y, x) for x, y in zip(lp, inv)]
    w_s = [_dot(y, x).astype(BF16) for x, y in zip(a_s, inv)]
    x_s = [_dot(x, y) for x, y in zip(a_ak, v_s)]
    x_s = [_dot(y, x) for x, y in zip(x_s, inv)]
    r_s = [x.astype(BF16) for x in r_s]

    outs = []
    pairs = range(N_PAIRS)
    s = [s_scr[p] for p in pairs]
    for ch in range(n_chunks):
        glast = jnp.exp(ctot[ch * c:ch * c + 1, :])
        at = lambda lst: lst[ch * N_PAIRS:(ch + 1) * N_PAIRS]
        u_s = [_dot_nt(w, sp) + x for w, sp, x in zip(at(w_s), s, at(x_s))]
        uv = [jnp.concatenate([u, v], axis=0) for u, v in zip(u_s, at(v_s))]
        rs = [_dot_nt(r, sp) for r, sp in zip(at(r_s), s)]
        o_s = [x + _dot(ar, y) for x, ar, y in zip(rs, at(a_r), uv)]
        upd = [_dot_tn(y, bk) for y, bk in zip(uv, at(bkh))]
        s = [s[p] * glast[:, p * PAIR:(p + 1) * PAIR] + upd[p] for p in pairs]
        outs.append(jnp.concatenate([x[:c] + x[c:] for x in o_s], axis=1))
    for p in pairs:
        s_scr[p] = s[p]
    o = jnp.concatenate(outs, axis=0)

    mu = head_sum(o) * (1.0 / HEAD_DIM)
    d = o - mu
    var = head_sum(d * d) * (1.0 / HEAD_DIM)
    bonus = head_sum(r_ * kf * rk_ref[...]) * v_
    y = d * lax.rsqrt(var + GN_EPS) * lg_ref[...] + lb_ref[...] + bonus
    o_ref[0] = (y * _silu(gr_ref[0])).astype(o_ref.dtype)

    @pl.when(t == pl.num_programs(1) - 1)
    def _():
        so_ref[0] = s_scr[...]


def _rwkv(zs, zl, gr, spm, spl, s0, prm, bd, tb):
    b, t, _ = zs.shape
    grid = (b, t // tb)
    seq = lambda w: pl.BlockSpec((1, tb, w), lambda i, j: (i, j, 0))
    per_b = lambda shp: pl.BlockSpec((1,) + shp, lambda i, j: (i,) + (0,) * len(shp))
    params = [prm["mu_main"], prm["mu_lora"], prm["w_lora"], prm["w0"], prm["a0"], prm["k_k"],
              prm["k_a"], prm["r_k"], prm["lnx_g"], prm["lnx_b"], bd]
    return pl.pallas_call(
        functools.partial(_rwkv_kernel, tb=tb),
        grid=grid,
        in_specs=[seq(SHIFT_MAIN), seq(2 * LORA), seq(D_RWKV),
                  per_b((1, SHIFT_MAIN)), per_b((1, 2 * LORA)), per_b((N_PAIRS, PAIR, PAIR))]
                 + [_const_spec(p.shape) for p in params],
        out_specs=[seq(D_RWKV), per_b((N_PAIRS, PAIR, PAIR))],
        out_shape=[jax.ShapeDtypeStruct((b, t, D_RWKV), F32),
                   jax.ShapeDtypeStruct((b, N_PAIRS, PAIR, PAIR), F32)],
        scratch_shapes=[pltpu.VMEM((1, SHIFT_MAIN), F32), pltpu.VMEM((1, 2 * LORA), F32),
                        pltpu.VMEM((N_PAIRS, PAIR, PAIR), F32)],
        compiler_params=pltpu.CompilerParams(
            dimension_semantics=("parallel", "arbitrary"), vmem_limit_bytes=VMEM_LIMIT),
        name="rwkv",
    )(zs, zl, gr, spm, spl, s0, *params)


def _attn_kernel(scal_ref, q_ref, k_ref, v_ref, ga_ref, sg_ref, o_ref,
                 acc_scr, *scr, bq, bkf, q_off):
    hp = pl.program_id(1)
    qi = pl.program_id(2)
    heads = range(HEADS_PER_STEP)
    lanes = lambda hh: slice(hh * PAIR, (hh + 1) * PAIR)
    slope = [scal_ref[HEADS_PER_STEP * hp + hh] * LOG2E for hh in heads]
    pipelined = bkf == bq

    @pl.when(qi == 0)
    def _():
        i_f = lax.broadcasted_iota(jnp.int32, (bkf, 2 * bq), 1)
        i_f = jnp.where(i_f >= bq, i_f - bq, i_f)
        j_f = lax.broadcasted_iota(jnp.int32, (bkf, 2 * bq), 0)
        rel = (j_f - i_f).astype(F32)
        i_d = lax.broadcasted_iota(jnp.int32, (bq, 2 * bq), 1)
        i_d = jnp.where(i_d >= bq, i_d - bq, i_d)
        j_d = lax.broadcasted_iota(jnp.int32, (bq, 2 * bq), 0)
        dist = jnp.abs(i_d - j_d).astype(F32)
        allowed = (j_d >> 6) <= (i_d >> 6)
        for hh in heads:
            diag = jnp.where(allowed, -slope[hh] * dist, NEG_INF)
            if pipelined:
                bias_scr, _, _ = scr
                bias_scr[hh, REL] = rel * slope[hh]
                bias_scr[hh, DIAG] = diag
                bias_scr[hh, NO_KEYS] = jnp.full((bq, 2 * bq), NEG_INF, F32)
            else:
                rel_scr, diag_scr = scr
                rel_scr[hh] = rel * slope[hh]
                diag_scr[hh] = diag

    first = lax.broadcasted_iota(jnp.int32, (bq, PAIR), 1) < HEAD_DIM
    qs = []
    for hh in heads:
        q = q_ref[0, :, lanes(hh)]
        zero = jnp.zeros_like(q)
        qs.append(jnp.concatenate([jnp.where(first, q, zero), jnp.where(first, zero, q)], axis=0))
    acc_scr[...] = jnp.zeros(acc_scr.shape, F32)
    q0 = q_off + qi * bq

    def scores(k0, size, hh):
        return lax.dot_general(k_ref[0, pl.ds(k0, size), lanes(hh)], qs[hh],
                               (((1,), (1,)), ((), ())), preferred_element_type=F32)

    def absorb(sts, k0, size, biases, offs, carry):
        new, upd = [], []
        for hh in heads:
            m, l = carry[2 * hh], carry[2 * hh + 1]
            st = sts[hh] + biases[hh]
            m_new = jnp.maximum(m, jnp.max(st, axis=0, keepdims=True) + offs[hh])
            p = jnp.exp2(st - (m_new - offs[hh]))
            alpha = jnp.exp2(m - m_new)
            new += [m_new, alpha * l + jnp.sum(p, axis=0, keepdims=True)]
            pv = lax.dot_general(v_ref[0, pl.ds(k0, size), lanes(hh)], p.astype(BF16),
                                 (((0,), (0,)), ((), ())), preferred_element_type=F32)
            upd.append((alpha, pv))
        for hh in heads:
            acc_scr[hh] = upd[hh][0] * acc_scr[hh] + upd[hh][1]
        return tuple(new)

    def rel_offs(k0):
        return [-slope[hh] * (q0 - k0).astype(F32) for hh in heads]

    init = (jnp.full((1, 2 * bq), NEG_INF, F32), jnp.zeros((1, 2 * bq), F32)) * HEADS_PER_STEP
    n_full = q0 // bkf
    if pipelined:
        bias_scr, st_a, st_b = scr

        def tile_args(t):
            kind = jnp.where(t < n_full, REL, jnp.where(t == n_full, DIAG, NO_KEYS))
            k0 = pl.multiple_of(jnp.minimum(t, n_full) * bkf, bkf)
            offs = [jnp.where(t < n_full, o, 0.0) for o in rel_offs(k0)]
            return kind, k0, offs

        for hh in heads:
            st_a[hh] = scores(0, bkf, hh)

        def body(i, carry):
            kind0, k0, offs0 = tile_args(2 * i)
            kind1, k1, offs1 = tile_args(2 * i + 1)
            _, k2, _ = tile_args(2 * i + 2)
            for hh in heads:
                st_b[hh] = scores(k1, bkf, hh)
            carry = absorb([st_a[hh] for hh in heads], k0, bkf,
                           [bias_scr[hh, kind0] for hh in heads], offs0, carry)
            for hh in heads:
                st_a[hh] = scores(k2, bkf, hh)
            return absorb([st_b[hh] for hh in heads], k1, bkf,
                          [bias_scr[hh, kind1] for hh in heads], offs1, carry)

        carry = lax.fori_loop(0, (n_full + 2) // 2, body, init)
    else:
        rel_scr, diag_scr = scr

        def body(j, carry):
            k0 = pl.multiple_of(j * bkf, bkf)
            return absorb([scores(k0, bkf, hh) for hh in heads], k0, bkf,
                          [rel_scr[hh] for hh in heads], rel_offs(k0), carry)

        carry = lax.fori_loop(0, n_full, body, init)
        kd = pl.multiple_of(q0, bq)
        carry = absorb([scores(kd, bq, hh) for hh in heads], kd, bq,
                       [diag_scr[hh] for hh in heads], [0.0] * HEADS_PER_STEP, carry)

    lam = scal_ref[N_DIFF_HEADS]
    post = scal_ref[N_DIFF_HEADS + 1]
    outs = []
    for hh in heads:
        on = acc_scr[hh] / carry[2 * hh + 1]
        ot = on[:, :bq] - lam * on[:, bq:]
        ot = ot * lax.rsqrt(jnp.mean(ot * ot, axis=0, keepdims=True) + SUBLN_EPS)
        outs.append((ot * sg_ref[...] * post).T)
    o = jnp.concatenate(outs, axis=1)
    o_ref[0] = (o * _silu(ga_ref[0])).astype(o_ref.dtype)


def _attn(scal, qb, kb, vb, ga, subln_col, bq, bkf, q_off):
    b, tq, _ = qb.shape
    tk = kb.shape[1]
    assert q_off % bkf == 0 and (bq % bkf == 0 or tq == bq)
    assert q_off % CHUNK == 0 and bq % CHUNK == 0 and q_off + tq == tk
    hs = HEADS_PER_STEP
    grid = (b, N_DIFF_HEADS // hs, tq // bq)
    q_spec = pl.BlockSpec((1, bq, hs * PAIR), lambda i, h, qi: (i, qi, h))
    kv_spec = pl.BlockSpec((1, tk, hs * PAIR), lambda i, h, qi: (i, 0, h))
    kern = functools.partial(_attn_kernel, bq=bq, bkf=bkf, q_off=q_off)
    return pl.pallas_call(
        kern,
        grid=grid,
        in_specs=[pl.BlockSpec(memory_space=pltpu.SMEM), q_spec, kv_spec, kv_spec, q_spec,
                  _const_spec((PAIR, 1))],
        out_specs=q_spec,
        out_shape=jax.ShapeDtypeStruct((b, tq, D_DIFF), F32),
        scratch_shapes=[pltpu.VMEM((hs, PAIR, 2 * bq), F32)] + (
            [pltpu.VMEM((hs, 3, bq, 2 * bq), F32)] + [pltpu.VMEM((hs, bkf, 2 * bq), F32)] * 2
            if bkf == bq else
            [pltpu.VMEM((hs, bkf, 2 * bq), F32), pltpu.VMEM((hs, bq, 2 * bq), F32)]),
        compiler_params=pltpu.CompilerParams(
            dimension_semantics=("arbitrary", "arbitrary", "arbitrary"),
            vmem_limit_bytes=VMEM_LIMIT),
        name="attn",
    )(scal, qb, kb, vb, ga, subln_col)


def _merge_kernel(x_ref, or_ref, oa_ref, mg_ref, p_ref, wr_ref, wa_ref, wo_ref, pw_ref, pg_ref,
                  png_ref, y_ref):
    mg = mg_ref[...]
    u = (_sigmoid(mg[:, :D_MODEL]) * _dot(or_ref[...], wr_ref[...])
         + _sigmoid(mg[:, D_MODEL:]) * _dot(oa_ref[...], wa_ref[...]))
    x = x_ref[...] + _dot(u, wo_ref[...])
    e = _dot(p_ref[...], pw_ref[...])
    ms = jnp.mean(x * x, axis=-1, keepdims=True)
    hn = x * lax.rsqrt(ms + NORM_EPS) * png_ref[...]
    y_ref[...] = x + e * _sigmoid(_dot(hn, pg_ref[...]))


def _merge(x, o_r, o_a, mg, p, prm, tm):
    n = x.shape[0]
    row = lambda w: pl.BlockSpec((tm, w), lambda i: (i, 0))
    ws = [prm["w_br_r"], prm["w_br_a"], prm["w_out"], prm["ple_w"], prm["ple_gate_w"],
          prm["ple_norm_g"]]
    return pl.pallas_call(
        _merge_kernel,
        grid=(n // tm,),
        in_specs=[row(D_MODEL), row(D_RWKV), row(D_DIFF), row(2 * D_MODEL), row(PLE_DIM)]
                 + [_const_spec(w.shape) for w in ws],
        out_specs=row(D_MODEL),
        out_shape=jax.ShapeDtypeStruct((n, D_MODEL), F32),
        compiler_params=pltpu.CompilerParams(
            dimension_semantics=("parallel",), vmem_limit_bytes=VMEM_LIMIT),
        name="merge",
    )(x, o_r, o_a, mg, p, *ws)


def _layer_params(i, norm_g, w_in, shift_mu, decay_w0, decay_w2, iclr_a0, iclr_a2, k_k, k_a, r_k,
                  lnx_g, lnx_b, q_norm_g, k_norm_g, lambda_q1, lambda_k1, lambda_q2, lambda_k2,
                  subln_g, w_br_r, w_br_a, w_out, ple_w, ple_gate_w, ple_norm_g):
    w = w_in[i]
    s = SHIFT_MAIN
    e = s + 2 * LORA
    w_perm = jnp.concatenate([w[:, :s], w[:, e:], w[:, s:e]], axis=1).astype(BF16)
    zeros = jnp.zeros((LORA, D_RWKV), F32)
    w_lora = jnp.concatenate([jnp.concatenate([decay_w2[i], zeros], axis=1),
                              jnp.concatenate([zeros, iclr_a2[i]], axis=1)], axis=0).astype(BF16)
    lam_init = 0.8 - 0.6 * math.exp(-0.3 * i)
    lam = (jnp.exp(jnp.sum(lambda_q1[i] * lambda_k1[i]))
           - jnp.exp(jnp.sum(lambda_q2[i] * lambda_k2[i])) + lam_init)
    slopes = jnp.asarray([2.0 ** (-8.0 * (h + 1) / N_DIFF_HEADS) for h in range(N_DIFF_HEADS)], F32)
    scal = jnp.concatenate([slopes, jnp.stack([lam, jnp.asarray(1.0 - lam_init, F32)])]).astype(F32)
    r2 = lambda u: u.reshape(1, -1)
    return dict(
        norm_g=r2(norm_g[i]), w_perm=w_perm,
        qg=r2(jnp.tile(q_norm_g[i], 2 * N_DIFF_HEADS)), kg=r2(jnp.tile(k_norm_g[i], 2 * N_DIFF_HEADS)),
        mu_main=r2(shift_mu[i, :s]), mu_lora=r2(shift_mu[i, s:]), w_lora=w_lora,
        w0=r2(decay_w0[i]), a0=r2(iclr_a0[i]), k_k=r2(k_k[i]), k_a=r2(k_a[i]), r_k=r2(r_k[i]),
        lnx_g=r2(lnx_g[i]), lnx_b=r2(lnx_b[i]), scal=scal, subln_col=subln_g[i].reshape(-1, 1),
        w_br_r=w_br_r[i].astype(BF16), w_br_a=w_br_a[i].astype(BF16), w_out=w_out[i].astype(BF16),
        ple_w=ple_w[i].astype(BF16), ple_gate_w=ple_gate_w[i].astype(BF16),
        ple_norm_g=r2(ple_norm_g[i]))


def _pair_state(s):
    b = s.shape[0]
    s = s.reshape(b, N_PAIRS, 2, HEAD_DIM, HEAD_DIM)
    z = jnp.zeros_like(s[:, :, 0])
    top = jnp.concatenate([s[:, :, 0], z], axis=-1)
    bot = jnp.concatenate([z, s[:, :, 1]], axis=-1)
    return jnp.concatenate([top, bot], axis=-2)


def _unpair_state(sp):
    b = sp.shape[0]
    h0 = sp[:, :, :HEAD_DIM, :HEAD_DIM]
    h1 = sp[:, :, HEAD_DIM:, HEAD_DIM:]
    return jnp.stack([h0, h1], axis=2).reshape(b, 2 * N_PAIRS, HEAD_DIM, HEAD_DIM)


def _group_layer(x, p_emb, prm, bd, shift_prev, wkv_prev, cache, tm, bq, tb):
    b, t, _ = x.shape
    n = b * t
    zs, zl, gr, ga, mg, k, v, qb, kb, vb = _in_proj(
        x.reshape(n, D_MODEL), prm["norm_g"], prm["w_perm"], prm["qg"], prm["kg"], bd, tm)
    seq = lambda u: u.reshape(b, t, u.shape[-1])
    zs, zl = seq(zs), seq(zl)
    o_r, s_new = _rwkv(zs, zl, seq(gr), shift_prev[:, None, :SHIFT_MAIN],
                       shift_prev[:, None, SHIFT_MAIN:], _pair_state(wkv_prev), prm, bd, tb)
    kb, vb = seq(kb), seq(vb)
    if cache is None:
        k_all, v_all, q_off = kb, vb, 0
    else:
        ck, cv = cache
        past = ck.shape[1]
        k_all = jnp.concatenate([ck.reshape(b, past, D_DIFF).astype(BF16), kb], axis=1)
        v_all = jnp.concatenate([cv.reshape(b, past, D_DIFF).astype(BF16), vb], axis=1)
        q_off = past
    o_a = _attn(prm["scal"], seq(qb), k_all, v_all, seq(ga), prm["subln_col"], bq, ATTN_KEY_TILE,
                q_off)
    y = _merge(x.reshape(n, D_MODEL), o_r.reshape(n, D_RWKV), o_a.reshape(n, D_DIFF), mg,
               p_emb.reshape(n, PLE_DIM), prm, tm)
    shift_new = jnp.concatenate([zs[:, -1], zl[:, -1]], axis=-1)
    return (y.reshape(b, t, D_MODEL), k.reshape(b, t, N_DIFF_HEADS, 2, HEAD_DIM),
            v.reshape(b, t, N_DIFF_HEADS, 2 * HEAD_DIM), _unpair_state(s_new), shift_new)


def kernel(x_prompt, x_sample, p_prompt, p_sample, cache_k, cache_v, state_wkv, state_shift, norm_g, w_in, shift_mu, decay_w0, decay_w2, iclr_a0, iclr_a2, k_k, k_a, r_k, lnx_g, lnx_b, q_norm_g, k_norm_g, lambda_q1, lambda_k1, lambda_q2, lambda_k2, subln_g, w_br_r, w_br_a, w_out, ple_w, ple_gate_w, ple_norm_g):
    depth = w_in.shape[0]
    b_p = x_prompt.shape[0]
    hid = jnp.arange(D_DIFF) // HEAD_DIM
    bd = (hid[:, None] == hid[None, :]).astype(BF16)
    zero_shift = jnp.zeros((b_p, SHIFT_MAIN + 2 * LORA), F32)
    zero_wkv = jnp.zeros((b_p, 2 * N_PAIRS, HEAD_DIM, HEAD_DIM), F32)
    y_p, y_s = x_prompt, x_sample
    outs_p, outs_s = [], []
    for i in range(depth):
        prm = _layer_params(i, norm_g, w_in, shift_mu, decay_w0, decay_w2, iclr_a0, iclr_a2, k_k,
                            k_a, r_k, lnx_g, lnx_b, q_norm_g, k_norm_g, lambda_q1, lambda_k1,
                            lambda_q2, lambda_k2, subln_g, w_br_r, w_br_a, w_out, ple_w,
                            ple_gate_w, ple_norm_g)
        y_p, *rest_p = _group_layer(y_p, p_prompt[i], prm, bd, zero_shift, zero_wkv, None,
                                    tm=256, bq=256, tb=256)
        y_s, *rest_s = _group_layer(y_s, p_sample[i], prm, bd, state_shift[i], state_wkv[i],
                                    (cache_k[i], cache_v[i]), tm=256, bq=x_sample.shape[1],
                                    tb=x_sample.shape[1])
        outs_p.append(rest_p)
        outs_s.append(rest_s)
    stack = lambda outs, j: jnp.stack([o[j] for o in outs])
    return (y_p, y_s, stack(outs_p, 0), stack(outs_p, 1), stack(outs_p, 2), stack(outs_p, 3),
            stack(outs_s, 0), stack(outs_s, 1), stack(outs_s, 2), stack(outs_s, 3))
```

```python
import functools
import math

import jax
import jax.numpy as jnp
from jax import lax
from jax.experimental import pallas as pl
from jax.experimental.pallas import tpu as pltpu

F32 = jnp.float32
BF16 = jnp.bfloat16

D_MODEL = 1024
HEAD_DIM = 64
D_RWKV = 512
N_PAIRS = D_RWKV // (2 * HEAD_DIM)
PAIR = 2 * HEAD_DIM
LORA = 64
N_DIFF_HEADS = 4
D_DIFF = 512
PLE_DIM = 256
CHUNK = 64
NORM_EPS = 1e-6
GN_EPS = 64e-5
SUBLN_EPS = 1e-5
NEG_INF = -1e30
NO_KEYS = 2 * NEG_INF
SHIFT_MAIN = 3 * D_RWKV
C_ZS, C_ZL, C_GR, C_Q, C_K, C_V, C_GA, C_MG, C_END = (
    0, 1536, 1664, 2176, 2688, 3200, 3712, 4224, 6272)

HEADS_PER_STEP = 2
LOG2E = 1.4426950408889634
BIAS_LANES = 3
VMEM_LIMIT = 56 * 1024 * 1024


def _tiles(t):
    return dict(tm=min(512, t), bq=min(256, t), bkf=256, tb=min(256, t))


def _dot(a, b):
    return jnp.dot(a.astype(BF16), b.astype(BF16), preferred_element_type=F32)


def _dot_nt(a, b):
    return lax.dot_general(a.astype(BF16), b.astype(BF16), (((1,), (1,)), ((), ())),
                           preferred_element_type=F32)


def _dot_tn(a, b):
    return lax.dot_general(a.astype(BF16), b.astype(BF16), (((0,), (0,)), ((), ())),
                           preferred_element_type=F32)


def _split_bf16(a, terms):
    parts = []
    rem = a
    for _ in range(terms):
        part = rem.astype(BF16)
        parts.append(part)
        rem = rem - part.astype(F32)
    return parts


def _dot_exact_rhs(a, b_exact, terms):
    return sum(jnp.dot(p, b_exact, preferred_element_type=F32) for p in _split_bf16(a, terms))


def _dot_exact_lhs(a_exact, b, terms):
    return sum(jnp.dot(a_exact, p, preferred_element_type=F32) for p in _split_bf16(b, terms))


def _sigmoid(x):
    return 1.0 / (1.0 + jnp.exp(-x))


def _silu(x):
    return x * _sigmoid(x)


def _const_spec(shape):
    nd = len(shape)
    return pl.BlockSpec(shape, lambda *_: (0,) * nd)


def _layer_spec(arr, layer):
    nd = arr.ndim - 1
    return pl.BlockSpec((None,) + arr.shape[1:], lambda *_: (layer,) + (0,) * nd)


def _in_proj_kernel(x_ref, g_ref, w_ref, qg_ref, kg_ref, bd_ref,
                    zs_ref, zl_ref, gr_ref, ga_ref, mg_ref, k_ref, v_ref,
                    qb_ref, kb_ref, vb_ref):
    x = x_ref[...]
    ms = jnp.mean(x * x, axis=-1, keepdims=True)
    h = (x * lax.rsqrt(ms + NORM_EPS) * g_ref[...]).astype(BF16)

    def proj(lo, hi):
        return jnp.dot(h, w_ref[:, lo:hi], preferred_element_type=F32)

    zs_ref[...] = proj(C_ZS, C_ZL)
    zl_ref[...] = proj(C_ZL, C_GR)
    gr_ref[...] = proj(C_GR, C_Q).astype(BF16)
    ga_ref[...] = proj(C_GA, C_MG).astype(BF16)
    mg_ref[...] = proj(C_MG, C_END).astype(BF16)
    v = proj(C_V, C_GA)
    v_ref[...] = v
    vb_ref[...] = v.astype(BF16)

    bd = bd_ref[...]

    def head_norm(t, gain):
        ss = _dot_exact_rhs(t * t, bd, 2)
        return t * lax.rsqrt(ss * (1.0 / HEAD_DIM) + NORM_EPS) * gain

    q = head_norm(proj(C_Q, C_K), qg_ref[...])
    qb_ref[...] = (q * (LOG2E * HEAD_DIM ** -0.5)).astype(BF16)
    k = head_norm(proj(C_K, C_V), kg_ref[...])
    k_ref[...] = k
    kb_ref[...] = k.astype(BF16)


def _in_proj(x, prm, layer, tm):
    n = x.shape[0]
    row = lambda w: pl.BlockSpec((tm, w), lambda i: (i, 0))
    out_shapes = [
        jax.ShapeDtypeStruct((n, SHIFT_MAIN), F32),
        jax.ShapeDtypeStruct((n, 2 * LORA), F32),
        jax.ShapeDtypeStruct((n, D_RWKV), BF16),
        jax.ShapeDtypeStruct((n, D_DIFF), BF16),
        jax.ShapeDtypeStruct((n, 2 * D_MODEL), BF16),
        jax.ShapeDtypeStruct((n, D_DIFF), F32),
        jax.ShapeDtypeStruct((n, D_DIFF), F32),
        jax.ShapeDtypeStruct((n, D_DIFF), BF16),
        jax.ShapeDtypeStruct((n, D_DIFF), BF16),
        jax.ShapeDtypeStruct((n, D_DIFF), BF16),
    ]
    params = [prm["norm_g"], prm["w_in"], prm["qg"], prm["kg"]]
    return pl.pallas_call(
        _in_proj_kernel,
        grid=(n // tm,),
        in_specs=[row(D_MODEL)] + [_layer_spec(p, layer) for p in params]
                 + [_const_spec(prm["bd"].shape)],
        out_specs=[row(s.shape[1]) for s in out_shapes],
        out_shape=out_shapes,
        compiler_params=pltpu.CompilerParams(
            dimension_semantics=("parallel",), vmem_limit_bytes=VMEM_LIMIT),
        name="in_proj",
    )(x, *params, prm["bd"])


def _rwkv_kernel(zs_ref, zl_ref, gr_ref, spm_ref, spl_ref, s0_ref,
                 mum_ref, mul_ref, wl_ref, w0_ref, a0_ref, kk_ref, ka_ref, rk_ref,
                 lg_ref, lb_ref, bd_ref,
                 o_ref, so_ref,
                 cm_scr, cl_scr, s_scr, *, tb):
    t = pl.program_id(1)
    c = CHUNK
    n_chunks = tb // c

    @pl.when(t == 0)
    def _():
        cm_scr[...] = spm_ref[0]
        cl_scr[...] = spl_ref[0]
        s_scr[...] = s0_ref[0]

    row = lax.broadcasted_iota(jnp.int32, (tb, 1), 0)

    def shifted(z_ref, carry, mu_ref):
        z = z_ref[0]
        zp = jnp.where(row == 0, carry[...], pltpu.roll(z, 1, axis=0))
        carry[...] = z[tb - 1:tb, :]
        return z + (zp - z) * mu_ref[...]

    zm = shifted(zs_ref, cm_scr, mum_ref)
    zl = shifted(zl_ref, cl_scr, mul_ref)
    r_ = zm[:, :D_RWKV]
    k_ = zm[:, D_RWKV:2 * D_RWKV]
    v_ = zm[:, 2 * D_RWKV:]

    lane = lax.broadcasted_iota(jnp.int32, (tb, 2 * LORA), 1)
    lin = jnp.where(lane < LORA, jnp.tanh(zl), zl)
    lo = _dot(lin, wl_ref[...])
    wpre = w0_ref[...] + lo[:, :D_RWKV]
    apre = a0_ref[...] + lo[:, D_RWKV:]
    sp = jnp.maximum(-wpre, 0.0) + jnp.log(1.0 + jnp.exp(-jnp.abs(wpre)))
    lw = -jnp.exp(-sp - 0.5)
    a = _sigmoid(apre)

    bd = bd_ref[...]

    def head_sum(u):
        return _dot_exact_rhs(u, bd, 2)

    kkr = k_ * kk_ref[...]
    kk = kkr / jnp.maximum(jnp.sqrt(head_sum(kkr * kkr)), 1e-12)
    kf = k_ * (1.0 + (a - 1.0) * ka_ref[...])

    ri = lax.broadcasted_iota(jnp.int32, (tb, tb), 0)
    ci = lax.broadcasted_iota(jnp.int32, (tb, tb), 1)
    same = (ri >> 6) == (ci >> 6)
    tri = jnp.where(jnp.logical_and(same, ci <= ri), 1.0, 0.0).astype(BF16)
    cs = _dot_exact_lhs(tri, lw, 3)
    if n_chunks == 1:
        ctot = cs[c - 1:c, :]
    else:
        ctot = _dot_exact_lhs(jnp.where(same, 1.0, 0.0).astype(BF16), lw, 3)
    g = jnp.exp(cs)
    ginv = jnp.exp(-cs)
    gend = jnp.exp(ctot - cs)
    ka = kk * a
    a_t = -kk * jnp.exp(cs - lw)
    b_t = ka * ginv
    k_t = kf * ginv
    r_t = r_ * g
    b_h = ka * gend
    k_h = kf * gend

    lane_p = lax.broadcasted_iota(jnp.int32, (c, PAIR), 1)
    first = lane_p < HEAD_DIM

    def stack(u):
        return jnp.concatenate([jnp.where(first, u, 0.0), jnp.where(first, 0.0, u)], axis=0)

    r2 = lax.broadcasted_iota(jnp.int32, (2 * c, 2 * c), 0)
    c2 = lax.broadcasted_iota(jnp.int32, (2 * c, 2 * c), 1)
    strict = c2 < r2
    eye = jnp.where(c2 == r2, 1.0, 0.0)
    r4 =lax.broadcasted_iota(jnp.int32, (2 * c, 4 * c), 0)
    c4 = lax.broadcasted_iota(jnp.int32, (2 * c, 4 * c), 1)
    incl = jnp.where(c4 >= 2 * c, c4 - 2 * c, c4) <= r4

    blocks = [(ch, p) for ch in range(n_chunks) for p in range(N_PAIRS)]

    def gather(u):
        return [stack(u[ch * c:(ch + 1) * c, p * PAIR:(p + 1) * PAIR]) for ch, p in blocks]

    a_s, r_s, b_s, k_s, v_s = gather(a_t), gather(r_t), gather(b_t), gather(k_t), gather(v_)
    bkh = [jnp.concatenate([x, y], axis=0).astype(BF16) for x, y in zip(gather(b_h), gather(k_h))]
    gmat = [_dot_nt(jnp.concatenate([x, y], axis=0), jnp.concatenate([z, w], axis=0))
            for x, y, z, w in zip(a_s, r_s, b_s, k_s)]
    a_ak = [jnp.where(strict, gm[:2 * c, 2 * c:], 0.0) for gm in gmat]
    a_r = [jnp.where(incl, gm[2 * c:, :], 0.0).astype(BF16) for gm in gmat]
    lp = [jnp.where(strict, gm[:2 * c, :2 * c], 0.0) for gm in gmat]
    inv = [eye + x for x in lp]
    for _ in range(5):
        lp = [_dot(x, x) for x in lp]
        inv = [y + _dot(y, x) for x, y in zip(lp, inv)]
    w_s = [_dot(y, x).astype(BF16) for x, y in zip(a_s, inv)]
    x_s = [_dot(x, y) for x, y in zip(a_ak, v_s)]
    x_s = [_dot(y, x) for x, y in zip(x_s, inv)]
    r_s = [x.astype(BF16) for x in r_s]

    outs = []
    pairs = range(N_PAIRS)
    s = [s_scr[p] for p in pairs]
    for ch in range(n_chunks):
        glast = jnp.exp(ctot[ch * c:ch * c + 1, :])
        at = lambda lst: lst[ch * N_PAIRS:(ch + 1) * N_PAIRS]
        u_s = [_dot_nt(w, sp) + x for w, sp, x in zip(at(w_s), s, at(x_s))]
        uv = [jnp.concatenate([u, v], axis=0) for u, v in zip(u_s, at(v_s))]
        rs = [_dot_nt(r, sp) for r, sp in zip(at(r_s), s)]
        o_s = [x + _dot(ar, u) for x, ar, u in zip(rs, at(a_r), uv)]
        upd = [_dot_tn(u, bk) for u, bk in zip(uv, at(bkh))]
        s = [s[p] * glast[:, p * PAIR:(p + 1) * PAIR] + upd[p] for p in pairs]
        outs.append(jnp.concatenate([x[:c] + x[c:] for x in o_s], axis=1))
    for p in pairs:
        s_scr[p] = s[p]
    o = jnp.concatenate(outs, axis=0)

    mu = head_sum(o) * (1.0 / HEAD_DIM)
    d = o - mu
    var = head_sum(d * d) * (1.0 / HEAD_DIM)
    bonus = head_sum(r_ * kf * rk_ref[...]) * v_
    yn = d * lax.rsqrt(var + GN_EPS) * lg_ref[...] + lb_ref[...] + bonus
    o_ref[0] = (yn * _silu(gr_ref[0].astype(F32))).astype(o_ref.dtype)

    @pl.when(t == pl.num_programs(1) - 1)
    def _():
        so_ref[0] = s_scr[...]


def _rwkv(zs, zl, gr, spm, spl, s0, prm, layer, tb):
    b, t, _ = zs.shape
    seq = lambda w: pl.BlockSpec((1, tb, w), lambda i, j: (i, j, 0))
    per_b = lambda shp: pl.BlockSpec((1,) + shp, lambda i, j: (i,) + (0,) * len(shp))
    params = [prm["mu_main"], prm["mu_lora"], prm["w_lora"], prm["w0"], prm["a0"], prm["k_k"],
              prm["k_a"], prm["r_k"], prm["lnx_g"], prm["lnx_b"]]
    return pl.pallas_call(
        functools.partial(_rwkv_kernel, tb=tb),
        grid=(b, t // tb),
        in_specs=[seq(SHIFT_MAIN), seq(2 * LORA), seq(D_RWKV),
                  per_b((1, SHIFT_MAIN)), per_b((1, 2 * LORA)), per_b((N_PAIRS, PAIR, PAIR))]
                 + [_layer_spec(p, layer) for p in params] + [_const_spec(prm["bd"].shape)],
        out_specs=[seq(D_RWKV), per_b((N_PAIRS, PAIR, PAIR))],
        out_shape=[jax.ShapeDtypeStruct((b, t, D_RWKV), F32),
                   jax.ShapeDtypeStruct((b, N_PAIRS, PAIR, PAIR), F32)],
        scratch_shapes=[pltpu.VMEM((1, SHIFT_MAIN), F32), pltpu.VMEM((1, 2 * LORA), F32),
                        pltpu.VMEM((N_PAIRS, PAIR, PAIR), F32)],
        compiler_params=pltpu.CompilerParams(
            dimension_semantics=("parallel", "arbitrary"), vmem_limit_bytes=VMEM_LIMIT),
        name="rwkv",
    )(zs, zl, gr, spm, spl, s0, *params, prm["bd"])


def _attn_kernel(scal_ref, q_ref, k_ref, v_ref, ga_ref, sg_ref, o_ref,
                 acc_scr, diag_scr, *scr, layer, bq, bkf, q_off):
    hp = pl.program_id(1)
    qi = pl.program_id(2)
    heads = range(HEADS_PER_STEP)
    lanes = lambda hh: slice(hh * PAIR, (hh + 1) * PAIR)
    slope = [scal_ref[layer, HEADS_PER_STEP * hp + hh] * LOG2E for hh in heads]
    folded = bkf == bq
    if folded:
        kx_scr, qx_scr, st_a, st_b = scr
    else:
        rel_scr, = scr

    @pl.when(qi == 0)
    def _():
        i_d = lax.broadcasted_iota(jnp.int32, (bq, 2 * bq), 1)
        i_d = jnp.where(i_d >= bq, i_d - bq, i_d)
        j_d = lax.broadcasted_iota(jnp.int32, (bq, 2 * bq), 0)
        dist = jnp.abs(i_d - j_d).astype(F32)
        allowed = (j_d >> 6) <= (i_d >> 6)
        for hh in heads:
            diag_scr[hh] = jnp.where(allowed, -slope[hh] * dist, NEG_INF)
            if folded:
                def extra(rows, pos_first):
                    lane = lax.broadcasted_iota(jnp.int32, (rows, PAIR), 1)
                    pos = lax.broadcasted_iota(jnp.int32, (rows, PAIR), 0)
                    pos = jnp.where(pos >= bq, pos - bq, pos).astype(F32)
                    terms = _split_bf16(jnp.full((rows, PAIR), slope[hh], F32), BIAS_LANES)
                    term = terms[BIAS_LANES - 1].astype(F32)
                    for n in range(BIAS_LANES - 1):
                        here = jnp.logical_or(lane == n, lane == n + BIAS_LANES)
                        term = jnp.where(here, terms[n].astype(F32), term)
                    lo, hi = (pos, term) if pos_first else (term, -pos)
                    out = jnp.where(lane < BIAS_LANES, lo,
                                    jnp.where(lane < 2 * BIAS_LANES, hi, 0.0))
                    return out.astype(BF16)

                kx_scr[hh] = extra(bkf, True)
                qx_scr[hh] = extra(2 * bq, False)
            else:
                i_f =lax.broadcasted_iota(jnp.int32, (bkf, 2 * bq), 1)
                i_f = jnp.where(i_f >= bq, i_f - bq, i_f)
                j_f = lax.broadcasted_iota(jnp.int32, (bkf, 2 * bq), 0)
                rel_scr[hh] = (j_f - i_f).astype(F32) * slope[hh]

    first = lax.broadcasted_iota(jnp.int32, (bq, PAIR), 1) < HEAD_DIM
    qs = []
    for hh in heads:
        q = q_ref[0, :, lanes(hh)]
        zero = jnp.zeros_like(q)
        qs.append(jnp.concatenate([jnp.where(first, q, zero), jnp.where(first, zero, q)], axis=0))
    acc_scr[...] = jnp.zeros(acc_scr.shape, F32)
    q0 = q_off + qi * bq
    nt = (((1,), (1,)), ((), ()))

    def scores(k0, size, hh):
        return lax.dot_general(k_ref[0, pl.ds(k0, size), lanes(hh)], qs[hh], nt,
                               preferred_element_type=F32)

    def absorb(sts, k0, size, offs, carry):
        new, upd = [], []
        for hh in heads:
            m, l = carry[2 * hh], carry[2 * hh + 1]
            st = sts[hh]
            m_new = jnp.maximum(m, jnp.max(st, axis=0, keepdims=True) + offs[hh])
            p = jnp.exp2(st - (m_new - offs[hh]))
            alpha = jnp.exp2(m - m_new)
            new += [m_new, alpha * l + jnp.sum(p, axis=0, keepdims=True)]
            pv = lax.dot_general(v_ref[0, pl.ds(k0, size), lanes(hh)], p.astype(BF16),
                                 (((0,), (0,)), ((), ())), preferred_element_type=F32)
            upd.append((alpha, pv))
        for hh in heads:
            acc_scr[hh] = upd[hh][0] * acc_scr[hh] + upd[hh][1]
        return tuple(new)

    def rel_offs(k0):
        return [-slope[hh] * (q0 - k0).astype(F32) for hh in heads]

    init = (jnp.full((1, 2 * bq), NEG_INF, F32), jnp.zeros((1, 2 * bq), F32)) * HEADS_PER_STEP
    n_full = q0 // bkf
    kd = pl.multiple_of(q0, bq)
    no_offs = [0.0] * HEADS_PER_STEP
    if folded:
        qa =[jnp.concatenate([qs[hh], qx_scr[hh]], axis=1) for hh in heads]

        def scores_rel(k0, hh):
            ka = jnp.concatenate([k_ref[0, pl.ds(k0, bkf), lanes(hh)], kx_scr[hh]], axis=1)
            return lax.dot_general(ka, qa[hh], nt, preferred_element_type=F32)

        for hh in heads:
            st_a[hh] = scores_rel(0, hh)

        def body(i, carry):
            k0 = pl.multiple_of(2 * i * bkf, bkf)
            k1 = pl.multiple_of(k0 + bkf, bkf)
            k2 = pl.multiple_of(k1 + bkf, bkf)
            for hh in heads:
                st_b[hh] = scores_rel(k1, hh)
            carry = absorb([st_a[hh] for hh in heads], k0, bkf, rel_offs(k0), carry)
            for hh in heads:
                st_a[hh] = scores_rel(k2, hh)
            return absorb([st_b[hh] for hh in heads], k1, bkf, rel_offs(k1), carry)

        carry = lax.fori_loop(0, n_full // 2, body, init)
        sd = [scores(kd, bq, hh) + diag_scr[hh] for hh in heads]
        odd = (n_full & 1) == 1
        ko = pl.multiple_of(jnp.maximum(n_full - 1, 0) * bkf, bkf)
        pad = jnp.where(odd, 0.0, NO_KEYS)
        offs = [jnp.where(odd, o, 0.0) for o in rel_offs(ko)]
        carry = absorb([st_a[hh] + pad for hh in heads], ko, bkf, offs, carry)
        carry = absorb(sd, kd, bq, no_offs, carry)
    else:
        def body(j, carry):
            k0 = pl.multiple_of(j * bkf, bkf)
            return absorb([scores(k0, bkf, hh) + rel_scr[hh] for hh in heads], k0, bkf,
                          rel_offs(k0), carry)

        carry =lax.fori_loop(0, n_full, body, init)
        carry = absorb([scores(kd, bq, hh) + diag_scr[hh] for hh in heads], kd, bq, no_offs, carry)

    lam = scal_ref[layer, N_DIFF_HEADS]
    post = scal_ref[layer, N_DIFF_HEADS + 1]
    outs = []
    for hh in heads:
        on = acc_scr[hh] / carry[2 * hh + 1]
        ot = on[:, :bq] - lam * on[:, bq:]
        ot = ot * lax.rsqrt(jnp.mean(ot * ot, axis=0, keepdims=True) + SUBLN_EPS)
        outs.append((ot * sg_ref[...] * post).T)
    o = jnp.concatenate(outs, axis=1)
    o_ref[0] = (o * _silu(ga_ref[0].astype(F32))).astype(o_ref.dtype)


def _attn(qb, kb, vb, ga, prm, layer, bq, bkf, q_off):
    b, tq, _ = qb.shape
    tk = kb.shape[1]
    assert q_off % bkf == 0 and (bq % bkf == 0 or tq == bq)
    assert q_off % CHUNK == 0 and bq % CHUNK == 0 and q_off + tq == tk
    assert bq <= 256
    hs = HEADS_PER_STEP
    q_spec = pl.BlockSpec((1, bq, hs * PAIR), lambda i, h, qi: (i, qi, h))
    kv_spec = pl.BlockSpec((1, tk, hs * PAIR), lambda i, h, qi: (i, 0, h))
    if bkf == bq:
        extra = [pltpu.VMEM((hs, bkf, PAIR), BF16), pltpu.VMEM((hs, 2 * bq, PAIR), BF16),
                 pltpu.VMEM((hs, bkf, 2 * bq), F32), pltpu.VMEM((hs, bkf, 2 * bq), F32)]
    else:
        extra = [pltpu.VMEM((hs, bkf, 2 * bq), F32)]
    kern = functools.partial(_attn_kernel, layer=layer, bq=bq, bkf=bkf, q_off=q_off)
    return pl.pallas_call(
        kern,
        grid=(b, N_DIFF_HEADS // hs, tq // bq),
        in_specs=[pl.BlockSpec(memory_space=pltpu.SMEM), q_spec, kv_spec, kv_spec, q_spec,
                  _layer_spec(prm["subln_col"], layer)],
        out_specs=q_spec,
        out_shape=jax.ShapeDtypeStruct((b, tq, D_DIFF), F32),
        scratch_shapes=[pltpu.VMEM((hs, PAIR, 2 * bq), F32), pltpu.VMEM((hs, bq, 2 * bq), F32)]
                       + extra,
        compiler_params=pltpu.CompilerParams(
            dimension_semantics=("arbitrary", "arbitrary", "arbitrary"),
            vmem_limit_bytes=VMEM_LIMIT),
        name="attn",
    )(prm["scal"], qb, kb, vb, ga, prm["subln_col"])


def _merge_kernel(x_ref, or_ref, oa_ref, mg_ref, p_ref, wr_ref, wa_ref, wo_ref, pw_ref, pg_ref,
                  png_ref, y_ref):
    mg = mg_ref[...].astype(F32)
    u = (_sigmoid(mg[:, :D_MODEL]) * _dot(or_ref[...], wr_ref[...])
         + _sigmoid(mg[:, D_MODEL:]) * _dot(oa_ref[...], wa_ref[...]))
    x = x_ref[...] + _dot(u, wo_ref[...])
    e = _dot(p_ref[...], pw_ref[...])
    ms = jnp.mean(x * x, axis=-1, keepdims=True)
    hn = x * lax.rsqrt(ms + NORM_EPS) * png_ref[...]
    y_ref[...] = x + e * _sigmoid(_dot(hn, pg_ref[...]))


def _merge(x, o_r, o_a, mg, p_all, prm, layer, tm):
    n = x.shape[0]
    row = lambda w: pl.BlockSpec((tm, w), lambda i: (i, 0))
    p_spec = pl.BlockSpec((None, tm, PLE_DIM), lambda i: (layer, i, 0))
    ws = [prm["w_br_r"], prm["w_br_a"], prm["w_out"], prm["ple_w"], prm["ple_gate_w"],
          prm["ple_norm_g"]]
    return pl.pallas_call(
        _merge_kernel,
        grid=(n // tm,),
        in_specs=[row(D_MODEL), row(D_RWKV), row(D_DIFF), row(2 * D_MODEL), p_spec]
                 + [_layer_spec(w, layer) for w in ws],
        out_specs=row(D_MODEL),
        out_shape=jax.ShapeDtypeStruct((n, D_MODEL), F32),
        compiler_params=pltpu.CompilerParams(
            dimension_semantics=("parallel",), vmem_limit_bytes=VMEM_LIMIT),
        name="merge",
    )(x, o_r, o_a, mg, p_all, *ws)


def _prepare(norm_g, w_in, shift_mu, decay_w0, decay_w2, iclr_a0, iclr_a2, k_k, k_a, r_k, lnx_g,
             lnx_b, q_norm_g, k_norm_g, lambda_q1, lambda_k1, lambda_q2, lambda_k2, subln_g,
             w_br_r, w_br_a, w_out, ple_w, ple_gate_w, ple_norm_g):
    depth = w_in.shape[0]
    vec = lambda u: u.reshape(depth, 1, -1)
    zeros = jnp.zeros((depth, LORA, D_RWKV), F32)
    w_lora = jnp.concatenate([jnp.concatenate([decay_w2, zeros], axis=2),
                              jnp.concatenate([zeros, iclr_a2], axis=2)], axis=1).astype(BF16)
    lam_init = jnp.asarray([0.8 - 0.6 * math.exp(-0.3 * i) for i in range(depth)], F32)
    lam = (jnp.exp(jnp.sum(lambda_q1 * lambda_k1, axis=-1))
           - jnp.exp(jnp.sum(lambda_q2 * lambda_k2, axis=-1)) + lam_init)
    slopes = jnp.asarray([2.0 ** (-8.0 * (h + 1) / N_DIFF_HEADS) for h in range(N_DIFF_HEADS)], F32)
    scal = jnp.concatenate([jnp.broadcast_to(slopes, (depth, N_DIFF_HEADS)), lam[:, None],
                            (1.0 - lam_init)[:, None]], axis=1).astype(F32)
    hid = jnp.arange(D_DIFF) // HEAD_DIM
    return dict(
        norm_g=vec(norm_g), w_in=w_in.astype(BF16),
        qg=vec(jnp.tile(q_norm_g, (1, 2 * N_DIFF_HEADS))), kg=vec(jnp.tile(k_norm_g, (1, 2 * N_DIFF_HEADS))),
        mu_main=vec(shift_mu[:, :SHIFT_MAIN]), mu_lora=vec(shift_mu[:, SHIFT_MAIN:]), w_lora=w_lora,
        w0=vec(decay_w0), a0=vec(iclr_a0), k_k=vec(k_k), k_a=vec(k_a), r_k=vec(r_k),
        lnx_g=vec(lnx_g), lnx_b=vec(lnx_b), scal=scal, subln_col=subln_g.reshape(depth, PAIR, 1),
        w_br_r=w_br_r.astype(BF16), w_br_a=w_br_a.astype(BF16), w_out=w_out.astype(BF16),
        ple_w=ple_w.astype(BF16), ple_gate_w=ple_gate_w.astype(BF16), ple_norm_g=vec(ple_norm_g),
        bd=(hid[:, None] == hid[None, :]).astype(BF16))


def _pair_state(s):
    b = s.shape[0]
    s = s.reshape(b, N_PAIRS, 2, HEAD_DIM, HEAD_DIM)
    z = jnp.zeros_like(s[:, :, 0])
    top = jnp.concatenate([s[:, :, 0], z], axis=-1)
    bot = jnp.concatenate([z, s[:, :, 1]], axis=-1)
    return jnp.concatenate([top, bot], axis=-2)


def _unpair_state(sp):
    b = sp.shape[0]
    h0 = sp[:, :, :HEAD_DIM, :HEAD_DIM]
    h1 = sp[:, :, HEAD_DIM:, HEAD_DIM:]
    return jnp.stack([h0, h1], axis=2).reshape(b, 2 * N_PAIRS, HEAD_DIM, HEAD_DIM)


def _group_layer(x, p_all, prm, layer, shift_prev, wkv_prev, cache):
    b, t, _ = x.shape
    n = b * t
    tl = _tiles(t)
    zs, zl, gr, ga, mg, k, v, qb, kb, vb = _in_proj(x.reshape(n, D_MODEL), prm, layer, tl["tm"])
    seq = lambda u: u.reshape(b, t, u.shape[-1])
    zs, zl = seq(zs), seq(zl)
    o_r, s_new = _rwkv(zs, zl, seq(gr), shift_prev[:, None, :SHIFT_MAIN],
                       shift_prev[:, None, SHIFT_MAIN:], _pair_state(wkv_prev), prm, layer,
                       tl["tb"])
    kb, vb = seq(kb), seq(vb)
    if cache is None:
        k_all, v_all, q_off = kb, vb, 0
    else:
        ck, cv = cache
        past = ck.shape[1]
        k_all = jnp.concatenate([ck.reshape(b, past, D_DIFF).astype(BF16), kb], axis=1)
        v_all = jnp.concatenate([cv.reshape(b, past, D_DIFF).astype(BF16), vb], axis=1)
        q_off = past
    o_a = _attn(seq(qb), k_all, v_all, seq(ga), prm, layer, tl["bq"], tl["bkf"], q_off)
    y = _merge(x.reshape(n, D_MODEL), o_r.reshape(n, D_RWKV), o_a.reshape(n, D_DIFF), mg,
               p_all.reshape(p_all.shape[0], n, PLE_DIM), prm, layer, tl["tm"])
    shift_new = jnp.concatenate([zs[:, -1], zl[:, -1]], axis=-1)
    return (y.reshape(b, t, D_MODEL), k.reshape(b, t, N_DIFF_HEADS, 2, HEAD_DIM),
            v.reshape(b, t, N_DIFF_HEADS, 2 * HEAD_DIM), _unpair_state(s_new), shift_new)


def kernel(x_prompt, x_sample, p_prompt, p_sample, cache_k, cache_v, state_wkv, state_shift, norm_g, w_in, shift_mu, decay_w0, decay_w2, iclr_a0, iclr_a2, k_k, k_a, r_k, lnx_g, lnx_b, q_norm_g, k_norm_g, lambda_q1, lambda_k1, lambda_q2, lambda_k2, subln_g, w_br_r, w_br_a, w_out, ple_w, ple_gate_w, ple_norm_g):
    depth = w_in.shape[0]
    b_p = x_prompt.shape[0]
    prm = _prepare(norm_g, w_in, shift_mu, decay_w0, decay_w2, iclr_a0, iclr_a2, k_k, k_a, r_k,
                   lnx_g, lnx_b, q_norm_g, k_norm_g, lambda_q1, lambda_k1, lambda_q2, lambda_k2,
                   subln_g, w_br_r, w_br_a, w_out, ple_w, ple_gate_w, ple_norm_g)
    zero_shift = jnp.zeros((b_p, SHIFT_MAIN + 2 * LORA), F32)
    zero_wkv = jnp.zeros((b_p, 2 * N_PAIRS, HEAD_DIM, HEAD_DIM), F32)
    y_p, y_s = x_prompt, x_sample
    outs_p, outs_s = [], []
    for i in range(depth):
        y_p, *rest_p = _group_layer(y_p, p_prompt, prm, i, zero_shift, zero_wkv, None)
        y_s, *rest_s = _group_layer(y_s, p_sample, prm, i, state_shift[i], state_wkv[i],
                                    (cache_k[i], cache_v[i]))
        outs_p.append(rest_p)
        outs_s.append(rest_s)
    stack = lambda outs, j: jnp.stack([o[j] for o in outs])
    return (y_p, y_s, stack(outs_p, 0), stack(outs_p, 1), stack(outs_p, 2), stack(outs_p, 3),
            stack(outs_s, 0), stack(outs_s, 1), stack(outs_s, 2), stack(outs_s, 3))
```

```python
import functools
import math

import jax
import jax.numpy as jnp
from jax import lax
from jax.experimental import pallas as pl
from jax.experimental.pallas import tpu as pltpu

F32 = jnp.float32
BF16 = jnp.bfloat16

D_MODEL = 1024
HEAD_DIM = 64
D_RWKV = 512
N_PAIRS = D_RWKV // (2 * HEAD_DIM)
PAIR = 2 * HEAD_DIM
LORA = 64
N_DIFF_HEADS = 4
D_DIFF = 512
PLE_DIM = 256
CHUNK = 64
NORM_EPS = 1e-6
GN_EPS = 64e-5
SUBLN_EPS = 1e-5
NEG_INF = -1e30
SHIFT_MAIN = 3 * D_RWKV
C_ZS, C_ZL, C_GR, C_Q, C_K, C_V, C_GA, C_MG, C_END = (
    0, 1536, 1664, 2176, 2688, 3200, 3712, 4224, 6272)

HEADS_PER_STEP = 2
LOG2E = 1.4426950408889634
HEAD_SUM_TERMS = 1
BIAS_LANES = 3
VMEM_LIMIT = 56 * 1024 * 1024


def _tiles(t):
    return dict(tm=512 if t >= 512 else 256, bq=min(256, t), bkf=256, tb=min(256, t))


def _dot(a, b):
    return jnp.dot(a.astype(BF16), b.astype(BF16), preferred_element_type=F32)


def _dot_nt(a, b):
    return lax.dot_general(a.astype(BF16), b.astype(BF16), (((1,), (1,)), ((), ())),
                           preferred_element_type=F32)


def _dot_tn(a, b):
    return lax.dot_general(a.astype(BF16), b.astype(BF16), (((0,), (0,)), ((), ())),
                           preferred_element_type=F32)


def _split_bf16(a, terms):
    parts = []
    rem = a
    for _ in range(terms):
        part = rem.astype(BF16)
        parts.append(part)
        rem = rem - part.astype(F32)
    return parts


def _dot_exact_rhs(a, b_exact, terms):
    return sum(jnp.dot(p, b_exact, preferred_element_type=F32) for p in _split_bf16(a, terms))


def _dot_exact_lhs(a_exact, b, terms):
    return sum(jnp.dot(a_exact, p, preferred_element_type=F32) for p in _split_bf16(b, terms))


def _sigmoid(x):
    return 1.0 / (1.0 + jnp.exp(-x))


def _silu(x):
    return x * _sigmoid(x)


def _const_spec(shape):
    nd = len(shape)
    return pl.BlockSpec(shape, lambda *_: (0,) * nd)


def _layer_spec(arr, layer):
    nd = arr.ndim - 1
    return pl.BlockSpec((None,) + arr.shape[1:], lambda *_: (layer,) + (0,) * nd)


def _in_proj_kernel(x_ref, g_ref, w_ref, qg_ref, kg_ref, bd_ref,
                    zs_ref, zl_ref, gr_ref, ga_ref, mg_ref, k_ref, v_ref,
                    qb_ref, kb_ref, vb_ref):
    x = x_ref[...]
    ms = jnp.mean(x * x, axis=-1, keepdims=True)
    h = (x * lax.rsqrt(ms + NORM_EPS) * g_ref[...]).astype(BF16)

    def proj(lo, hi):
        return jnp.dot(h, w_ref[:, lo:hi], preferred_element_type=F32)

    zs_ref[...] = proj(C_ZS, C_ZL)
    zl_ref[...] = proj(C_ZL, C_GR)
    gr_ref[...] = proj(C_GR, C_Q).astype(BF16)
    ga_ref[...] = proj(C_GA, C_MG).astype(BF16)
    mg_ref[...] = proj(C_MG, C_END).astype(BF16)
    v = proj(C_V, C_GA)
    v_ref[...] = v
    vb_ref[...] = v.astype(BF16)

    bd = bd_ref[...]

    def head_norm(t, gain):
        ss = _dot_exact_rhs(t * t, bd, HEAD_SUM_TERMS)
        return t * lax.rsqrt(ss * (1.0 / HEAD_DIM) + NORM_EPS) * gain

    q = head_norm(proj(C_Q, C_K), qg_ref[...])
    qb_ref[...] = (q * (LOG2E * HEAD_DIM ** -0.5)).astype(BF16)
    k = head_norm(proj(C_K, C_V), kg_ref[...])
    k_ref[...] = k
    kb_ref[...] = k.astype(BF16)


def _in_proj(x, prm, layer, tm):
    n = x.shape[0]
    row = lambda w: pl.BlockSpec((tm, w), lambda i: (i, 0))
    out_shapes = [
        jax.ShapeDtypeStruct((n, SHIFT_MAIN), F32),
        jax.ShapeDtypeStruct((n, 2 * LORA), F32),
        jax.ShapeDtypeStruct((n, D_RWKV), BF16),
        jax.ShapeDtypeStruct((n, D_DIFF), BF16),
        jax.ShapeDtypeStruct((n, 2 * D_MODEL), BF16),
        jax.ShapeDtypeStruct((n, D_DIFF), F32),
        jax.ShapeDtypeStruct((n, D_DIFF), F32),
        jax.ShapeDtypeStruct((n, D_DIFF), BF16),
        jax.ShapeDtypeStruct((n, D_DIFF), BF16),
        jax.ShapeDtypeStruct((n, D_DIFF), BF16),
    ]
    params = [prm["norm_g"], prm["w_in"], prm["qg"], prm["kg"]]
    return pl.pallas_call(
        _in_proj_kernel,
        grid=(n // tm,),
        in_specs=[row(D_MODEL)] + [_layer_spec(p, layer) for p in params]
                 + [_const_spec(prm["bd"].shape)],
        out_specs=[row(s.shape[1]) for s in out_shapes],
        out_shape=out_shapes,
        compiler_params=pltpu.CompilerParams(
            dimension_semantics=("parallel",), vmem_limit_bytes=VMEM_LIMIT),
        name="in_proj",
    )(x, *params, prm["bd"])


def _rwkv_kernel(zs_ref, zl_ref, gr_ref, spm_ref, spl_ref, s0_ref,
                 mum_ref, mul_ref, wl_ref, w0_ref, a0_ref, kk_ref, ka_ref, rk_ref,
                 lg_ref, lb_ref, bd_ref,
                 o_ref, so_ref,
                 cm_scr, cl_scr, s_scr, *, tb):
    t = pl.program_id(1)
    c = CHUNK
    n_chunks = tb // c

    @pl.when(t == 0)
    def _():
        cm_scr[...] = spm_ref[0]
        cl_scr[...] = spl_ref[0]
        s_scr[...] = s0_ref[0]

    row = lax.broadcasted_iota(jnp.int32, (tb, 1), 0)

    def shifted(z_ref, carry, mu_ref):
        z = z_ref[0]
        zp = jnp.where(row == 0, carry[...], pltpu.roll(z, 1, axis=0))
        carry[...] = z[tb - 1:tb, :]
        return z + (zp - z) * mu_ref[...]

    zm = shifted(zs_ref, cm_scr, mum_ref)
    zl = shifted(zl_ref, cl_scr, mul_ref)
    r_ = zm[:, :D_RWKV]
    k_ = zm[:, D_RWKV:2 * D_RWKV]
    v_ = zm[:, 2 * D_RWKV:]

    lane = lax.broadcasted_iota(jnp.int32, (tb, 2 * LORA), 1)
    lin = jnp.where(lane < LORA, jnp.tanh(zl), zl)
    lo = _dot(lin, wl_ref[...])
    wpre = w0_ref[...] + lo[:, :D_RWKV]
    apre = a0_ref[...] + lo[:, D_RWKV:]
    sp = jnp.maximum(-wpre, 0.0) + jnp.log(1.0 + jnp.exp(-jnp.abs(wpre)))
    lw = -jnp.exp(-sp - 0.5)
    a = _sigmoid(apre)

    bd = bd_ref[...]

    def head_sum(u):
        return _dot_exact_rhs(u, bd, HEAD_SUM_TERMS)

    kkr = k_ * kk_ref[...]
    kk = kkr / jnp.maximum(jnp.sqrt(head_sum(kkr * kkr)), 1e-12)
    kf = k_ * (1.0 + (a - 1.0) * ka_ref[...])

    ri = lax.broadcasted_iota(jnp.int32, (tb, tb), 0)
    ci = lax.broadcasted_iota(jnp.int32, (tb, tb), 1)
    same = (ri >> 6) == (ci >> 6)
    tri = jnp.where(jnp.logical_and(same, ci <= ri), 1.0, 0.0).astype(BF16)
    cs = _dot_exact_lhs(tri, lw, 3)
    if n_chunks == 1:
        ctot = cs[c - 1:c, :]
    else:
        ctot = _dot_exact_lhs(jnp.where(same, 1.0, 0.0).astype(BF16), lw, 3)
    g = jnp.exp(cs)
    ginv = jnp.exp(-cs)
    gend = jnp.exp(ctot - cs)
    ka = kk * a
    a_t = -kk * jnp.exp(cs - lw)
    b_t = ka * ginv
    k_t = kf * ginv
    r_t = r_ * g
    b_h = ka * gend
    k_h = kf * gend

    lane_p = lax.broadcasted_iota(jnp.int32, (c, PAIR), 1)
    first = lane_p < HEAD_DIM

    def stack(u):
        return jnp.concatenate([jnp.where(first, u, 0.0), jnp.where(first, 0.0, u)], axis=0)

    r2 = lax.broadcasted_iota(jnp.int32, (2 * c, 2 * c), 0)
    c2 = lax.broadcasted_iota(jnp.int32, (2 * c, 2 * c), 1)
    strict = c2 < r2
    eye = jnp.where(c2 == r2, 1.0, 0.0)
    r4 =lax.broadcasted_iota(jnp.int32, (2 * c, 4 * c), 0)
    c4 = lax.broadcasted_iota(jnp.int32, (2 * c, 4 * c), 1)
    incl = jnp.where(c4 >= 2 * c, c4 - 2 * c, c4) <= r4

    blocks = [(ch, p) for ch in range(n_chunks) for p in range(N_PAIRS)]

    def gather(u):
        return [stack(u[ch * c:(ch + 1) * c, p * PAIR:(p + 1) * PAIR]) for ch, p in blocks]

    a_s, r_s, b_s, k_s, v_s = gather(a_t), gather(r_t), gather(b_t), gather(k_t), gather(v_)
    bkh = [jnp.concatenate([x, y], axis=0).astype(BF16) for x, y in zip(gather(b_h), gather(k_h))]
    gmat = [_dot_nt(jnp.concatenate([x, y], axis=0), jnp.concatenate([z, w], axis=0))
            for x, y, z, w in zip(a_s, r_s, b_s, k_s)]
    a_ak = [jnp.where(strict, gm[:2 * c, 2 * c:], 0.0) for gm in gmat]
    a_r = [jnp.where(incl, gm[2 * c:, :], 0.0).astype(BF16) for gm in gmat]
    lp = [jnp.where(strict, gm[:2 * c, :2 * c], 0.0) for gm in gmat]
    inv = [eye + x for x in lp]
    for _ in range(5):
        lp = [_dot(x, x) for x in lp]
        inv = [y + _dot(y, x) for x, y in zip(lp, inv)]
    w_s = [_dot(y, x).astype(BF16) for x, y in zip(a_s, inv)]
    x_s = [_dot(x, y) for x, y in zip(a_ak, v_s)]
    x_s = [_dot(y, x) for x, y in zip(x_s, inv)]
    r_s = [x.astype(BF16) for x in r_s]

    outs = []
    pairs = range(N_PAIRS)
    s = [s_scr[p] for p in pairs]
    for ch in range(n_chunks):
        glast = jnp.exp(ctot[ch * c:ch * c + 1, :])
        at = lambda lst: lst[ch * N_PAIRS:(ch + 1) * N_PAIRS]
        u_s = [_dot_nt(w, sp) + x for w, sp, x in zip(at(w_s), s, at(x_s))]
        uv = [jnp.concatenate([u, v], axis=0) for u, v in zip(u_s, at(v_s))]
        rs = [_dot_nt(r, sp) for r, sp in zip(at(r_s), s)]
        o_s = [x + _dot(ar, u) for x, ar, u in zip(rs, at(a_r), uv)]
        upd = [_dot_tn(u, bk) for u, bk in zip(uv, at(bkh))]
        s = [s[p] * glast[:, p * PAIR:(p + 1) * PAIR] + upd[p] for p in pairs]
        outs.append(jnp.concatenate([x[:c] + x[c:] for x in o_s], axis=1))
    for p in pairs:
        s_scr[p] = s[p]
    o = jnp.concatenate(outs, axis=0)

    mu = head_sum(o) * (1.0 / HEAD_DIM)
    d = o - mu
    var = head_sum(d * d) * (1.0 / HEAD_DIM)
    bonus = head_sum(r_ * kf * rk_ref[...]) * v_
    yn = d * lax.rsqrt(var + GN_EPS) * lg_ref[...] + lb_ref[...] + bonus
    o_ref[0] = (yn * _silu(gr_ref[0].astype(F32))).astype(o_ref.dtype)

    @pl.when(t == pl.num_programs(1) - 1)
    def _():
        so_ref[0] = s_scr[...]


def _rwkv(zs, zl, gr, spm, spl, s0, prm, layer, tb):
    b, t, _ = zs.shape
    seq = lambda w: pl.BlockSpec((1, tb, w), lambda i, j: (i, j, 0))
    per_b = lambda shp: pl.BlockSpec((1,) + shp, lambda i, j: (i,) + (0,) * len(shp))
    params = [prm["mu_main"], prm["mu_lora"], prm["w_lora"], prm["w0"], prm["a0"], prm["k_k"],
              prm["k_a"], prm["r_k"], prm["lnx_g"], prm["lnx_b"]]
    return pl.pallas_call(
        functools.partial(_rwkv_kernel, tb=tb),
        grid=(b, t // tb),
        in_specs=[seq(SHIFT_MAIN), seq(2 * LORA), seq(D_RWKV),
                  per_b((1, SHIFT_MAIN)), per_b((1, 2 * LORA)), per_b((N_PAIRS, PAIR, PAIR))]
                 + [_layer_spec(p, layer) for p in params] + [_const_spec(prm["bd"].shape)],
        out_specs=[seq(D_RWKV), per_b((N_PAIRS, PAIR, PAIR))],
        out_shape=[jax.ShapeDtypeStruct((b, t, D_RWKV), F32),
                   jax.ShapeDtypeStruct((b, N_PAIRS, PAIR, PAIR), F32)],
        scratch_shapes=[pltpu.VMEM((1, SHIFT_MAIN), F32), pltpu.VMEM((1, 2 * LORA), F32),
                        pltpu.VMEM((N_PAIRS, PAIR, PAIR), F32)],
        compiler_params=pltpu.CompilerParams(
            dimension_semantics=("parallel", "arbitrary"), vmem_limit_bytes=VMEM_LIMIT),
        name="rwkv",
    )(zs, zl, gr, spm, spl, s0, *params, prm["bd"])


def _attn_kernel(scal_ref, q_ref, k_ref, v_ref, *refs, layer, bq, bkf, q_off, has_cache):
    if has_cache:
        kp_ref, vp_ref, *refs = refs
    else:
        kp_ref, vp_ref = k_ref, v_ref
    ga_ref, sg_ref, o_ref, acc_scr, m_scr, l_scr, diag_scr, *scr = refs
    hp = pl.program_id(1)
    qi = pl.program_id(2)
    heads = range(HEADS_PER_STEP)
    lanes = lambda hh: slice(hh * PAIR, (hh + 1) * PAIR)
    slope = [scal_ref[layer, HEADS_PER_STEP * hp + hh] * LOG2E for hh in heads]
    folded = bkf == bq
    if folded:
        kx_scr, qx_scr, st_a, st_b = scr
    else:
        rel_scr, = scr

    @pl.when(qi == 0)
    def _():
        i_d = lax.broadcasted_iota(jnp.int32, (bq, 2 * bq), 1)
        i_d = jnp.where(i_d >= bq, i_d - bq, i_d)
        j_d = lax.broadcasted_iota(jnp.int32, (bq, 2 * bq), 0)
        dist = jnp.abs(i_d - j_d).astype(F32)
        allowed = (j_d >> 6) <= (i_d >> 6)
        for hh in heads:
            diag_scr[hh] = jnp.where(allowed, -slope[hh] * dist, NEG_INF)
            if folded:
                def extra(rows, pos_first):
                    lane = lax.broadcasted_iota(jnp.int32, (rows, PAIR), 1)
                    pos = lax.broadcasted_iota(jnp.int32, (rows, PAIR), 0)
                    pos = jnp.where(pos >= bq, pos - bq, pos).astype(F32)
                    terms = _split_bf16(jnp.full((rows, PAIR), slope[hh], F32), BIAS_LANES)
                    term = terms[BIAS_LANES - 1].astype(F32)
                    for n in range(BIAS_LANES - 1):
                        here = jnp.logical_or(lane == n, lane == n + BIAS_LANES)
                        term = jnp.where(here, terms[n].astype(F32), term)
                    lo, hi = (pos, term) if pos_first else (term, -pos)
                    out = jnp.where(lane < BIAS_LANES, lo,
                                    jnp.where(lane < 2 * BIAS_LANES, hi, 0.0))
                    return out.astype(BF16)

                kx_scr[hh] = extra(bkf, True)
                qx_scr[hh] = extra(2 * bq, False)
            else:
                i_f =lax.broadcasted_iota(jnp.int32, (bkf, 2 * bq), 1)
                i_f = jnp.where(i_f >= bq, i_f - bq, i_f)
                j_f = lax.broadcasted_iota(jnp.int32, (bkf, 2 * bq), 0)
                rel_scr[hh] = (j_f - i_f).astype(F32) * slope[hh]

    first = lax.broadcasted_iota(jnp.int32, (bq, PAIR), 1) < HEAD_DIM
    qs = []
    for hh in heads:
        q = q_ref[0, :, lanes(hh)]
        zero = jnp.zeros_like(q)
        qs.append(jnp.concatenate([jnp.where(first, q, zero), jnp.where(first, zero, q)], axis=0))
    acc_scr[...] = jnp.zeros(acc_scr.shape, F32)
    m_scr[...] = jnp.full(m_scr.shape, NEG_INF, F32)
    l_scr[...] = jnp.zeros(l_scr.shape, F32)
    q0 = q_off + qi * bq
    nt = (((1,), (1,)), ((), ()))

    def tile(ref, k0, size, hh):
        return ref[0, pl.ds(k0, size), lanes(hh)].astype(BF16)

    def scores(ref, k0, size, hh):
        return lax.dot_general(tile(ref, k0, size, hh), qs[hh], nt, preferred_element_type=F32)

    def absorb(sts, ref, k0, size, offs):
        upd = []
        for hh in heads:
            m, st = m_scr[hh], sts[hh]
            m_new = jnp.maximum(m, jnp.max(st, axis=0, keepdims=True) + offs[hh])
            p = jnp.exp2(st - (m_new - offs[hh]))
            alpha = jnp.exp2(m - m_new)
            m_scr[hh] = m_new
            l_scr[hh] = alpha * l_scr[hh] + jnp.sum(p, axis=0, keepdims=True)
            pv = lax.dot_general(tile(ref, k0, size, hh), p.astype(BF16),
                                 (((0,), (0,)), ((), ())), preferred_element_type=F32)
            upd.append((alpha, pv))
        for hh in heads:
            acc_scr[hh] = upd[hh][0] * acc_scr[hh] + upd[hh][1]

    def rel_offs(k0):
        return [-slope[hh] * (q0 - k0).astype(F32) for hh in heads]

    n_full = q0 // bkf
    kd = pl.multiple_of(qi * bq + (0 if has_cache else q_off), bq)
    no_offs = [0.0] * HEADS_PER_STEP
    if folded:
        qa =[jnp.concatenate([qs[hh], qx_scr[hh]], axis=1) for hh in heads]

        def scores_rel(k0, hh):
            ka = jnp.concatenate([tile(kp_ref, k0, bkf, hh), kx_scr[hh]], axis=1)
            return lax.dot_general(ka, qa[hh], nt, preferred_element_type=F32)

        for hh in heads:
            st_a[hh] = scores_rel(0, hh)

        def body(i, carry):
            k0 = pl.multiple_of(2 * i * bkf, bkf)
            k1 = pl.multiple_of(k0 + bkf, bkf)
            k2 = pl.multiple_of(k1 + bkf, bkf)
            for hh in heads:
                st_b[hh] = scores_rel(k1, hh)
            absorb([st_a[hh] for hh in heads], vp_ref, k0, bkf, rel_offs(k0))
            for hh in heads:
                st_a[hh] = scores_rel(k2, hh)
            absorb([st_b[hh] for hh in heads], vp_ref, k1, bkf, rel_offs(k1))
            return carry

        lax.fori_loop(0, n_full // 2, body, 0)

        @pl.when((n_full & 1) == 1)
        def _():
            ko = pl.multiple_of((n_full - 1) * bkf, bkf)
            absorb([st_a[hh] for hh in heads], vp_ref, ko, bkf, rel_offs(ko))
    else:
        def body(j, carry):
            k0 = pl.multiple_of(j * bkf, bkf)
            absorb([scores(kp_ref, k0, bkf, hh) + rel_scr[hh] for hh in heads], vp_ref, k0, bkf,
                   rel_offs(k0))
            return carry

        lax.fori_loop(0, n_full, body, 0)
    absorb([scores(k_ref, kd, bq, hh) + diag_scr[hh] for hh in heads], v_ref, kd, bq, no_offs)

    lam = scal_ref[layer, N_DIFF_HEADS]
    post = scal_ref[layer, N_DIFF_HEADS + 1]
    outs = []
    for hh in heads:
        on = acc_scr[hh] / l_scr[hh]
        ot = on[:, :bq] - lam * on[:, bq:]
        ot = ot * lax.rsqrt(jnp.mean(ot * ot, axis=0, keepdims=True) + SUBLN_EPS)
        outs.append((ot * sg_ref[...] * post).T)
    o = jnp.concatenate(outs, axis=1)
    o_ref[0] = (o * _silu(ga_ref[0].astype(F32))).astype(o_ref.dtype)


def _attn(qb, kb, vb, ga, prm, layer, bq, bkf, cache):
    b, tq, _ = qb.shape
    q_off = 0 if cache is None else cache[0].shape[1]
    assert q_off % bkf == 0 and (bq % bkf == 0 or tq == bq)
    assert q_off % CHUNK == 0 and bq % CHUNK == 0
    assert bq <= 256
    hs = HEADS_PER_STEP
    q_spec = pl.BlockSpec((1, bq, hs * PAIR), lambda i, h, qi: (i, qi, h))
    kv_spec = lambda tk: pl.BlockSpec((1, tk, hs * PAIR), lambda i, h, qi: (i, 0, h))
    past = [] if cache is None else list(cache)
    if bkf == bq:
        extra = [pltpu.VMEM((hs, bkf, PAIR), BF16), pltpu.VMEM((hs, 2 * bq, PAIR), BF16),
                 pltpu.VMEM((hs, bkf, 2 * bq), F32), pltpu.VMEM((hs, bkf, 2 * bq), F32)]
    else:
        extra = [pltpu.VMEM((hs, bkf, 2 * bq), F32)]
    kern = functools.partial(_attn_kernel, layer=layer, bq=bq, bkf=bkf, q_off=q_off,
                             has_cache=cache is not None)
    stat = pltpu.VMEM((hs, 1, 2 * bq), F32)
    return pl.pallas_call(
        kern,
        grid=(b, N_DIFF_HEADS // hs, tq // bq),
        in_specs=[pl.BlockSpec(memory_space=pltpu.SMEM), q_spec, kv_spec(tq), kv_spec(tq)]
                 + [kv_spec(q_off)] * len(past)
                 + [q_spec, _layer_spec(prm["subln_col"], layer)],
        out_specs=q_spec,
        out_shape=jax.ShapeDtypeStruct((b, tq, D_DIFF), F32),
        scratch_shapes=[pltpu.VMEM((hs, PAIR, 2 * bq), F32), stat, stat,
                        pltpu.VMEM((hs, bq, 2 * bq), F32)] + extra,
        compiler_params=pltpu.CompilerParams(
            dimension_semantics=("arbitrary", "arbitrary", "arbitrary"),
            vmem_limit_bytes=VMEM_LIMIT),
        name="attn",
    )(prm["scal"], qb, kb, vb, *past, ga, prm["subln_col"])


def _merge_kernel(x_ref, or_ref, oa_ref, mg_ref, p_ref, wr_ref, wa_ref, wo_ref, pw_ref, pg_ref,
                  png_ref, y_ref):
    mg = mg_ref[...].astype(F32)
    u = (_sigmoid(mg[:, :D_MODEL]) * _dot(or_ref[...], wr_ref[...])
         + _sigmoid(mg[:, D_MODEL:]) * _dot(oa_ref[...], wa_ref[...]))
    x = x_ref[...] + _dot(u, wo_ref[...])
    e = _dot(p_ref[...], pw_ref[...])
    ms = jnp.mean(x * x, axis=-1, keepdims=True)
    hn = x * lax.rsqrt(ms + NORM_EPS) * png_ref[...]
    y_ref[...] = x + e * _sigmoid(_dot(hn, pg_ref[...]))


def _merge(x, o_r, o_a, mg, p_all, prm, layer, tm):
    n = x.shape[0]
    row = lambda w: pl.BlockSpec((tm, w), lambda i: (i, 0))
    p_spec = pl.BlockSpec((None, tm, PLE_DIM), lambda i: (layer, i, 0))
    ws = [prm["w_br_r"], prm["w_br_a"], prm["w_out"], prm["ple_w"], prm["ple_gate_w"],
          prm["ple_norm_g"]]
    return pl.pallas_call(
        _merge_kernel,
        grid=(n // tm,),
        in_specs=[row(D_MODEL), row(D_RWKV), row(D_DIFF), row(2 * D_MODEL), p_spec]
                 + [_layer_spec(w, layer) for w in ws],
        out_specs=row(D_MODEL),
        out_shape=jax.ShapeDtypeStruct((n, D_MODEL), F32),
        compiler_params=pltpu.CompilerParams(
            dimension_semantics=("parallel",), vmem_limit_bytes=VMEM_LIMIT),
        name="merge",
    )(x, o_r, o_a, mg, p_all, *ws)


def _prepare(norm_g, w_in, shift_mu, decay_w0, decay_w2, iclr_a0, iclr_a2, k_k, k_a, r_k, lnx_g,
             lnx_b, q_norm_g, k_norm_g, lambda_q1, lambda_k1, lambda_q2, lambda_k2, subln_g,
             w_br_r, w_br_a, w_out, ple_w, ple_gate_w, ple_norm_g):
    depth = w_in.shape[0]
    vec = lambda u: u.reshape(depth, 1, -1)
    zeros = jnp.zeros((depth, LORA, D_RWKV), F32)
    w_lora = jnp.concatenate([jnp.concatenate([decay_w2, zeros], axis=2),
                              jnp.concatenate([zeros, iclr_a2], axis=2)], axis=1).astype(BF16)
    lam_init = jnp.asarray([0.8 - 0.6 * math.exp(-0.3 * i) for i in range(depth)], F32)
    lam = (jnp.exp(jnp.sum(lambda_q1 * lambda_k1, axis=-1))
           - jnp.exp(jnp.sum(lambda_q2 * lambda_k2, axis=-1)) + lam_init)
    slopes = jnp.asarray([2.0 ** (-8.0 * (h + 1) / N_DIFF_HEADS) for h in range(N_DIFF_HEADS)], F32)
    scal = jnp.concatenate([jnp.broadcast_to(slopes, (depth, N_DIFF_HEADS)), lam[:, None],
                            (1.0 - lam_init)[:, None]], axis=1).astype(F32)
    hid = jnp.arange(D_DIFF) // HEAD_DIM
    return dict(
        norm_g=vec(norm_g), w_in=w_in.astype(BF16),
        qg=vec(jnp.tile(q_norm_g, (1, 2 * N_DIFF_HEADS))), kg=vec(jnp.tile(k_norm_g, (1, 2 * N_DIFF_HEADS))),
        mu_main=vec(shift_mu[:, :SHIFT_MAIN]), mu_lora=vec(shift_mu[:, SHIFT_MAIN:]), w_lora=w_lora,
        w0=vec(decay_w0), a0=vec(iclr_a0), k_k=vec(k_k), k_a=vec(k_a), r_k=vec(r_k),
        lnx_g=vec(lnx_g), lnx_b=vec(lnx_b), scal=scal, subln_col=subln_g.reshape(depth, PAIR, 1),
        w_br_r=w_br_r.astype(BF16), w_br_a=w_br_a.astype(BF16), w_out=w_out.astype(BF16),
        ple_w=ple_w.astype(BF16), ple_gate_w=ple_gate_w.astype(BF16), ple_norm_g=vec(ple_norm_g),
        bd=(hid[:, None] == hid[None, :]).astype(BF16))


def _pair_state(s):
    b = s.shape[0]
    s = s.reshape(b, N_PAIRS, 2, HEAD_DIM, HEAD_DIM)
    z = jnp.zeros_like(s[:, :, 0])
    top = jnp.concatenate([s[:, :, 0], z], axis=-1)
    bot = jnp.concatenate([z, s[:, :, 1]], axis=-1)
    return jnp.concatenate([top, bot], axis=-2)


def _unpair_state(sp):
    b = sp.shape[0]
    h0 = sp[:, :, :HEAD_DIM, :HEAD_DIM]
    h1 = sp[:, :, HEAD_DIM:, HEAD_DIM:]
    return jnp.stack([h0, h1], axis=2).reshape(b, 2 * N_PAIRS, HEAD_DIM, HEAD_DIM)


def _group_layer(x, p_all, prm, layer, shift_prev, wkv_prev, cache):
    b, t, _ = x.shape
    n = b * t
    tl = _tiles(t)
    zs, zl, gr, ga, mg, k, v, qb, kb, vb = _in_proj(x.reshape(n, D_MODEL), prm, layer, tl["tm"])
    seq = lambda u: u.reshape(b, t, u.shape[-1])
    zs, zl = seq(zs), seq(zl)
    o_r, s_new = _rwkv(zs, zl, seq(gr), shift_prev[:, None, :SHIFT_MAIN],
                       shift_prev[:, None, SHIFT_MAIN:], _pair_state(wkv_prev), prm, layer,
                       tl["tb"])
    if cache is not None:
        cache = tuple(u.reshape(b, u.shape[1], D_DIFF) for u in cache)
    o_a = _attn(seq(qb), seq(kb), seq(vb), seq(ga), prm, layer, tl["bq"], tl["bkf"], cache)
    y = _merge(x.reshape(n, D_MODEL), o_r.reshape(n, D_RWKV), o_a.reshape(n, D_DIFF), mg,
               p_all.reshape(p_all.shape[0], n, PLE_DIM), prm, layer, tl["tm"])
    shift_new = jnp.concatenate([zs[:, -1], zl[:, -1]], axis=-1)
    return (y.reshape(b, t, D_MODEL), k.reshape(b, t, N_DIFF_HEADS, 2, HEAD_DIM),
            v.reshape(b, t, N_DIFF_HEADS, 2 * HEAD_DIM), _unpair_state(s_new), shift_new)


def kernel(x_prompt, x_sample, p_prompt, p_sample, cache_k, cache_v, state_wkv, state_shift, norm_g, w_in, shift_mu, decay_w0, decay_w2, iclr_a0, iclr_a2, k_k, k_a, r_k, lnx_g, lnx_b, q_norm_g, k_norm_g, lambda_q1, lambda_k1, lambda_q2, lambda_k2, subln_g, w_br_r, w_br_a, w_out, ple_w, ple_gate_w, ple_norm_g):
    depth = w_in.shape[0]
    b_p = x_prompt.shape[0]
    prm = _prepare(norm_g, w_in, shift_mu, decay_w0, decay_w2, iclr_a0, iclr_a2, k_k, k_a, r_k,
                   lnx_g, lnx_b, q_norm_g, k_norm_g, lambda_q1, lambda_k1, lambda_q2, lambda_k2,
                   subln_g, w_br_r, w_br_a, w_out, ple_w, ple_gate_w, ple_norm_g)
    zero_shift = jnp.zeros((b_p, SHIFT_MAIN + 2 * LORA), F32)
    zero_wkv = jnp.zeros((b_p, 2 * N_PAIRS, HEAD_DIM, HEAD_DIM), F32)
    y_p, y_s = x_prompt, x_sample
    outs_p, outs_s = [], []
    for i in range(depth):
        y_p, *rest_p = _group_layer(y_p, p_prompt, prm, i, zero_shift, zero_wkv, None)
        y_s, *rest_s = _group_layer(y_s, p_sample, prm, i, state_shift[i], state_wkv[i],
                                    (cache_k[i], cache_v[i]))
        outs_p.append(rest_p)
        outs_s.append(rest_s)
    stack = lambda outs, j: jnp.stack([o[j] for o in outs])
    return (y_p, y_s, stack(outs_p, 0), stack(outs_p, 1), stack(outs_p, 2), stack(outs_p, 3),
            stack(outs_s, 0), stack(outs_s, 1), stack(outs_s, 2), stack(outs_s, 3))
```

```python
import functools
import math

import jax
import jax.numpy as jnp
from jax import lax
from jax.experimental import pallas as pl
from jax.experimental.pallas import tpu as pltpu

F32 = jnp.float32
BF16 = jnp.bfloat16

D_MODEL = 1024
HEAD_DIM = 64
D_RWKV = 512
N_PAIRS = D_RWKV // (2 * HEAD_DIM)
PAIR = 2 * HEAD_DIM
LORA = 64
N_DIFF_HEADS = 4
D_DIFF = 512
PLE_DIM = 256
CHUNK = 64
NORM_EPS = 1e-6
GN_EPS = 64e-5
SUBLN_EPS = 1e-5
NEG_INF = -1e30
NO_KEYS = 2 * NEG_INF
SHIFT_MAIN = 3 * D_RWKV
C_ZS, C_ZL, C_GR, C_Q, C_K, C_V, C_GA, C_MG, C_END = (
    0, 1536, 1664, 2176, 2688, 3200, 3712, 4224, 6272)

HEADS_PER_STEP = 2
LOG2E = 1.4426950408889634
PREP_GROUP = 16
HEAD_SUM_SLAB = 256
HEAD_SUM_TERMS = 1
BIAS_LANES = 3
VMEM_LIMIT = 56 * 1024 * 1024


def _tiles(t):
    return dict(tm=512 if t >= 512 else 256, bq=min(256, t), bkf=256, tb=min(256, t))


def _dot(a, b):
    return jnp.dot(a.astype(BF16), b.astype(BF16), preferred_element_type=F32)


def _dot_nt(a, b):
    return lax.dot_general(a.astype(BF16), b.astype(BF16), (((1,), (1,)), ((), ())),
                           preferred_element_type=F32)


def _dot_tn(a, b):
    return lax.dot_general(a.astype(BF16), b.astype(BF16), (((0,), (0,)), ((), ())),
                           preferred_element_type=F32)


def _head_sums(u, bd, terms):
    w = bd.shape[0]
    return jnp.concatenate([_dot_exact_rhs(u[:, i:i + w], bd, terms)
                            for i in range(0, u.shape[1], w)], axis=1)


def _split_bf16(a, terms):
    parts = []
    rem = a
    for _ in range(terms):
        part = rem.astype(BF16)
        parts.append(part)
        rem = rem - part.astype(F32)
    return parts


def _dot_exact_rhs(a, b_exact, terms):
    return sum(jnp.dot(p, b_exact, preferred_element_type=F32) for p in _split_bf16(a, terms))


def _dot_exact_lhs(a_exact, b, terms):
    return sum(jnp.dot(a_exact, p, preferred_element_type=F32) for p in _split_bf16(b, terms))


def _sigmoid(x):
    return 1.0 / (1.0 + jnp.exp(-x))


def _silu(x):
    return x * _sigmoid(x)


def _const_spec(shape):
    nd = len(shape)
    return pl.BlockSpec(shape, lambda *_: (0,) * nd)


def _layer_spec(arr, layer):
    nd = arr.ndim - 1
    return pl.BlockSpec((None,) + arr.shape[1:], lambda *_: (layer,) + (0,) * nd)


def _in_proj_kernel(x_ref, g_ref, w_ref, qg_ref, kg_ref, bd_ref,
                    zs_ref, zl_ref, gr_ref, ga_ref, mg_ref, k_ref, v_ref,
                    qb_ref, kb_ref, vb_ref):
    x = x_ref[...]
    ms = jnp.mean(x * x, axis=-1, keepdims=True)
    h = (x * lax.rsqrt(ms + NORM_EPS) * g_ref[...]).astype(BF16)

    def proj(lo, hi):
        return jnp.dot(h, w_ref[:, lo:hi], preferred_element_type=F32)

    zs_ref[...] = proj(C_ZS, C_ZL)
    zl_ref[...] = proj(C_ZL, C_GR)
    gr_ref[...] = proj(C_GR, C_Q).astype(BF16)
    ga_ref[...] = proj(C_GA, C_MG).astype(BF16)
    mg_ref[...] = proj(C_MG, C_END).astype(BF16)
    v = proj(C_V, C_GA)
    v_ref[...] = v
    vb_ref[...] = v.astype(BF16)

    bd = bd_ref[...]

    def head_norm(t, gain):
        ss = _head_sums(t * t, bd, HEAD_SUM_TERMS)
        return t * lax.rsqrt(ss * (1.0 / HEAD_DIM) + NORM_EPS) * gain

    q = head_norm(proj(C_Q, C_K), qg_ref[...])
    qb_ref[...] = (q * (LOG2E * HEAD_DIM ** -0.5)).astype(BF16)
    k = head_norm(proj(C_K, C_V), kg_ref[...])
    k_ref[...] = k
    kb_ref[...] = k.astype(BF16)


def _in_proj(x, prm, layer, tm):
    n = x.shape[0]
    row = lambda w: pl.BlockSpec((tm, w), lambda i: (i, 0))
    out_shapes = [
        jax.ShapeDtypeStruct((n, SHIFT_MAIN), F32),
        jax.ShapeDtypeStruct((n, 2 * LORA), F32),
        jax.ShapeDtypeStruct((n, D_RWKV), BF16),
        jax.ShapeDtypeStruct((n, D_DIFF), BF16),
        jax.ShapeDtypeStruct((n, 2 * D_MODEL), BF16),
        jax.ShapeDtypeStruct((n, D_DIFF), F32),
        jax.ShapeDtypeStruct((n, D_DIFF), F32),
        jax.ShapeDtypeStruct((n, D_DIFF), BF16),
        jax.ShapeDtypeStruct((n, D_DIFF), BF16),
        jax.ShapeDtypeStruct((n, D_DIFF), BF16),
    ]
    params = [prm["norm_g"], prm["w_in"], prm["qg"], prm["kg"]]
    return pl.pallas_call(
        _in_proj_kernel,
        grid=(n // tm,),
        in_specs=[row(D_MODEL)] + [_layer_spec(p, layer) for p in params]
                 + [_const_spec(prm["bd"].shape)],
        out_specs=[row(s.shape[1]) for s in out_shapes],
        out_shape=out_shapes,
        compiler_params=pltpu.CompilerParams(
            dimension_semantics=("parallel",), vmem_limit_bytes=VMEM_LIMIT),
        name="in_proj",
    )(x, *params, prm["bd"])


def _rwkv_kernel(zs_ref, zl_ref, gr_ref, spm_ref, spl_ref, s0_ref,
                 mum_ref, mul_ref, wl_ref, w0_ref, a0_ref, kk_ref, ka_ref, rk_ref,
                 lg_ref, lb_ref, bd_ref,
                 o_ref, so_ref,
                 cm_scr, cl_scr, s_scr, *, tb):
    t = pl.program_id(1)
    c = CHUNK
    n_chunks = tb // c

    @pl.when(t == 0)
    def _():
        cm_scr[...] = spm_ref[0]
        cl_scr[...] = spl_ref[0]
        s_scr[...] = s0_ref[0]

    row = lax.broadcasted_iota(jnp.int32, (tb, 1), 0)

    def shifted(z_ref, carry, mu_ref):
        z = z_ref[0]
        zp = jnp.where(row == 0, carry[...], pltpu.roll(z, 1, axis=0))
        carry[...] = z[tb - 1:tb, :]
        return z + (zp - z) * mu_ref[...]

    zm = shifted(zs_ref, cm_scr, mum_ref)
    zl = shifted(zl_ref, cl_scr, mul_ref)
    r_ = zm[:, :D_RWKV]
    k_ = zm[:, D_RWKV:2 * D_RWKV]
    v_ = zm[:, 2 * D_RWKV:]

    lane = lax.broadcasted_iota(jnp.int32, (tb, 2 * LORA), 1)
    lin = jnp.where(lane < LORA, jnp.tanh(zl), zl)
    lo = _dot(lin, wl_ref[...])
    wpre = w0_ref[...] + lo[:, :D_RWKV]
    apre = a0_ref[...] + lo[:, D_RWKV:]
    sp = jnp.maximum(-wpre, 0.0) + jnp.log(1.0 + jnp.exp(-jnp.abs(wpre)))
    lw = -jnp.exp(-sp - 0.5)
    a = _sigmoid(apre)

    bd = bd_ref[...]

    def head_sum(u):
        return _head_sums(u, bd, HEAD_SUM_TERMS)

    kkr = k_ * kk_ref[...]
    kk = kkr / jnp.maximum(jnp.sqrt(head_sum(kkr * kkr)), 1e-12)
    kf = k_ * (1.0 + (a - 1.0) * ka_ref[...])

    ri = lax.broadcasted_iota(jnp.int32, (tb, tb), 0)
    ci = lax.broadcasted_iota(jnp.int32, (tb, tb), 1)
    same = (ri >> 6) == (ci >> 6)
    tri = jnp.where(jnp.logical_and(same, ci <= ri), 1.0, 0.0).astype(BF16)
    cs = _dot_exact_lhs(tri, lw, 3)
    if n_chunks == 1:
        ctot = cs[c - 1:c, :]
    else:
        ctot = _dot_exact_lhs(jnp.where(same, 1.0, 0.0).astype(BF16), lw, 3)
    g = jnp.exp(cs)
    ginv = jnp.exp(-cs)
    gend = jnp.exp(ctot - cs)
    ka = kk * a
    a_t = -kk * jnp.exp(cs - lw)
    b_t = ka * ginv
    k_t = kf * ginv
    r_t = r_ * g
    b_h = ka * gend
    k_h = kf * gend

    lane_p = lax.broadcasted_iota(jnp.int32, (c, PAIR), 1)
    first = lane_p < HEAD_DIM

    def stack(u):
        return jnp.concatenate([jnp.where(first, u, 0.0), jnp.where(first, 0.0, u)], axis=0)

    r2 = lax.broadcasted_iota(jnp.int32, (2 * c, 2 * c), 0)
    c2 = lax.broadcasted_iota(jnp.int32, (2 * c, 2 * c), 1)
    strict = c2 < r2
    eye = jnp.where(c2 == r2, 1.0, 0.0)
    r4 =lax.broadcasted_iota(jnp.int32, (2 * c, 4 * c), 0)
    c4 = lax.broadcasted_iota(jnp.int32, (2 * c, 4 * c), 1)
    incl = jnp.where(c4 >= 2 * c, c4 - 2 * c, c4) <= r4

    def prepare(blocks):
        def gather(u):
            return [stack(u[ch * c:(ch + 1) * c, p * PAIR:(p + 1) * PAIR]) for ch, p in blocks]

        a_s, r_s, b_s, k_s, v_s = gather(a_t), gather(r_t), gather(b_t), gather(k_t), gather(v_)
        bkh = [jnp.concatenate([x, y], axis=0).astype(BF16)
               for x, y in zip(gather(b_h), gather(k_h))]
        gmat = [_dot_nt(jnp.concatenate([x, y], axis=0), jnp.concatenate([z, w], axis=0))
                for x, y, z, w in zip(a_s, r_s, b_s, k_s)]
        a_ak = [jnp.where(strict, gm[:2 * c, 2 * c:], 0.0) for gm in gmat]
        a_r = [jnp.where(incl, gm[2 * c:, :], 0.0).astype(BF16) for gm in gmat]
        lp = [jnp.where(strict, gm[:2 * c, :2 * c], 0.0) for gm in gmat]
        inv = [eye + x for x in lp]
        for _ in range(5):
            lp = [_dot(x, x) for x in lp]
            inv = [y + _dot(y, x) for x, y in zip(lp, inv)]
        w_s = [_dot(y, x).astype(BF16) for x, y in zip(a_s, inv)]
        x_s = [_dot(x, y) for x, y in zip(a_ak, v_s)]
        x_s = [_dot(y, x) for x, y in zip(x_s, inv)]
        return w_s, x_s, a_r, [x.astype(BF16) for x in r_s], v_s, bkh

    blocks = [(ch, p) for ch in range(n_chunks) for p in range(N_PAIRS)]
    groups = [prepare(blocks[i:i + PREP_GROUP]) for i in range(0, len(blocks), PREP_GROUP)]
    w_s, x_s, a_r, r_s, v_s, bkh = (sum(parts, []) for parts in zip(*groups))

    outs = []
    pairs = range(N_PAIRS)
    s = [s_scr[p] for p in pairs]
    for ch in range(n_chunks):
        glast = jnp.exp(ctot[ch * c:ch * c + 1, :])
        at = lambda lst: lst[ch * N_PAIRS:(ch + 1) * N_PAIRS]
        u_s = [_dot_nt(w, sp) + x for w, sp, x in zip(at(w_s), s, at(x_s))]
        uv = [jnp.concatenate([u, v], axis=0) for u, v in zip(u_s, at(v_s))]
        rs = [_dot_nt(r, sp) for r, sp in zip(at(r_s), s)]
        o_s = [x + _dot(ar, u) for x, ar, u in zip(rs, at(a_r), uv)]
        upd = [_dot_tn(u, bk) for u, bk in zip(uv, at(bkh))]
        s = [s[p] * glast[:, p * PAIR:(p + 1) * PAIR] + upd[p] for p in pairs]
        outs.append(jnp.concatenate([x[:c] + x[c:] for x in o_s], axis=1))
    for p in pairs:
        s_scr[p] = s[p]
    o = jnp.concatenate(outs, axis=0)

    mu = head_sum(o) * (1.0 / HEAD_DIM)
    d = o - mu
    var = head_sum(d * d) * (1.0 / HEAD_DIM)
    bonus = head_sum(r_ * kf * rk_ref[...]) * v_
    yn = d * lax.rsqrt(var + GN_EPS) * lg_ref[...] + lb_ref[...] + bonus
    o_ref[0] = (yn * _silu(gr_ref[0].astype(F32))).astype(o_ref.dtype)

    @pl.when(t == pl.num_programs(1) - 1)
    def _():
        so_ref[0] = s_scr[...]


def _rwkv(zs, zl, gr, spm, spl, s0, prm, layer, tb):
    b, t, _ = zs.shape
    seq = lambda w: pl.BlockSpec((1, tb, w), lambda i, j: (i, j, 0))
    per_b = lambda shp: pl.BlockSpec((1,) + shp, lambda i, j: (i,) + (0,) * len(shp))
    params = [prm["mu_main"], prm["mu_lora"], prm["w_lora"], prm["w0"], prm["a0"], prm["k_k"],
              prm["k_a"], prm["r_k"], prm["lnx_g"], prm["lnx_b"]]
    return pl.pallas_call(
        functools.partial(_rwkv_kernel, tb=tb),
        grid=(b, t // tb),
        in_specs=[seq(SHIFT_MAIN), seq(2 * LORA), seq(D_RWKV),
                  per_b((1, SHIFT_MAIN)), per_b((1, 2 * LORA)), per_b((N_PAIRS, PAIR, PAIR))]
                 + [_layer_spec(p, layer) for p in params] + [_const_spec(prm["bd"].shape)],
        out_specs=[seq(D_RWKV), per_b((N_PAIRS, PAIR, PAIR))],
        out_shape=[jax.ShapeDtypeStruct((b, t, D_RWKV), F32),
                   jax.ShapeDtypeStruct((b, N_PAIRS, PAIR, PAIR), F32)],
        scratch_shapes=[pltpu.VMEM((1, SHIFT_MAIN), F32), pltpu.VMEM((1, 2 * LORA), F32),
                        pltpu.VMEM((N_PAIRS, PAIR, PAIR), F32)],
        compiler_params=pltpu.CompilerParams(
            dimension_semantics=("parallel", "arbitrary"), vmem_limit_bytes=VMEM_LIMIT),
        name="rwkv",
    )(zs, zl, gr, spm, spl, s0, *params, prm["bd"])


def _attn_kernel(scal_ref, q_ref, k_ref, v_ref, *refs, layer, bq, bkf, q_off, has_cache):
    if has_cache:
        kp_ref, vp_ref, *refs = refs
    else:
        kp_ref, vp_ref = k_ref, v_ref
    ga_ref, sg_ref, o_ref, acc_scr, diag_scr, *scr = refs
    hp = pl.program_id(1)
    qi = pl.program_id(2)
    heads = range(HEADS_PER_STEP)
    lanes = lambda hh: slice(hh * PAIR, (hh + 1) * PAIR)
    slope = [scal_ref[layer, HEADS_PER_STEP * hp + hh] * LOG2E for hh in heads]
    folded = bkf == bq
    if folded:
        kx_scr, qx_scr, st_a, st_b = scr
    else:
        rel_scr, = scr

    @pl.when(qi == 0)
    def _():
        i_d = lax.broadcasted_iota(jnp.int32, (bq, 2 * bq), 1)
        i_d = jnp.where(i_d >= bq, i_d - bq, i_d)
        j_d = lax.broadcasted_iota(jnp.int32, (bq, 2 * bq), 0)
        dist = jnp.abs(i_d - j_d).astype(F32)
        allowed = (j_d >> 6) <= (i_d >> 6)
        for hh in heads:
            diag_scr[hh] = jnp.where(allowed, -slope[hh] * dist, NEG_INF)
            if folded:
                def extra(rows, pos_first):
                    lane = lax.broadcasted_iota(jnp.int32, (rows, PAIR), 1)
                    pos = lax.broadcasted_iota(jnp.int32, (rows, PAIR), 0)
                    pos = jnp.where(pos >= bq, pos - bq, pos).astype(F32)
                    terms = _split_bf16(jnp.full((rows, PAIR), slope[hh], F32), BIAS_LANES)
                    term = terms[BIAS_LANES - 1].astype(F32)
                    for n in range(BIAS_LANES - 1):
                        here = jnp.logical_or(lane == n, lane == n + BIAS_LANES)
                        term = jnp.where(here, terms[n].astype(F32), term)
                    lo, hi = (pos, term) if pos_first else (term, -pos)
                    out = jnp.where(lane < BIAS_LANES, lo,
                                    jnp.where(lane < 2 * BIAS_LANES, hi, 0.0))
                    return out.astype(BF16)

                kx_scr[hh] = extra(bkf, True)
                qx_scr[hh] = extra(2 * bq, False)
            else:
                i_f =lax.broadcasted_iota(jnp.int32, (bkf, 2 * bq), 1)
                i_f = jnp.where(i_f >= bq, i_f - bq, i_f)
                j_f = lax.broadcasted_iota(jnp.int32, (bkf, 2 * bq), 0)
                rel_scr[hh] = (j_f - i_f).astype(F32) * slope[hh]

    first = lax.broadcasted_iota(jnp.int32, (bq, PAIR), 1) < HEAD_DIM
    qs = []
    for hh in heads:
        q = q_ref[0, :, lanes(hh)]
        zero = jnp.zeros_like(q)
        qs.append(jnp.concatenate([jnp.where(first, q, zero), jnp.where(first, zero, q)], axis=0))
    acc_scr[...] = jnp.zeros(acc_scr.shape, F32)
    q0 = q_off + qi * bq
    nt = (((1,), (1,)), ((), ()))

    def tile(ref, k0, size, hh):
        return ref[0, pl.ds(k0, size), lanes(hh)].astype(BF16)

    def scores(ref, k0, size, hh):
        return lax.dot_general(tile(ref, k0, size, hh), qs[hh], nt, preferred_element_type=F32)

    def absorb(sts, ref, k0, size, offs, carry):
        new, upd = [], []
        for hh in heads:
            m, l, st = carry[2 * hh], carry[2 * hh + 1], sts[hh]
            m_new = jnp.maximum(m, jnp.max(st, axis=0, keepdims=True) + offs[hh])
            p = jnp.exp2(st - (m_new - offs[hh]))
            alpha = jnp.exp2(m - m_new)
            new += [m_new, alpha * l + jnp.sum(p, axis=0, keepdims=True)]
            pv = lax.dot_general(tile(ref, k0, size, hh), p.astype(BF16),
                                 (((0,), (0,)), ((), ())), preferred_element_type=F32)
            upd.append((alpha, pv))
        for hh in heads:
            acc_scr[hh] = upd[hh][0] * acc_scr[hh] + upd[hh][1]
        return tuple(new)

    def rel_offs(k0):
        return [-slope[hh] * (q0 - k0).astype(F32) for hh in heads]

    carry = (jnp.full((1, 2 * bq), NEG_INF, F32), jnp.zeros((1, 2 * bq), F32)) * HEADS_PER_STEP
    n_full = q0 // bkf
    kd = pl.multiple_of(qi * bq + (0 if has_cache else q_off), bq)
    no_offs = [0.0] * HEADS_PER_STEP
    if folded:
        qa =[jnp.concatenate([qs[hh], qx_scr[hh]], axis=1) for hh in heads]

        def scores_rel(k0, hh):
            ka = jnp.concatenate([tile(kp_ref, k0, bkf, hh), kx_scr[hh]], axis=1)
            return lax.dot_general(ka, qa[hh], nt, preferred_element_type=F32)

        for hh in heads:
            st_a[hh] = scores_rel(0, hh)

        def body(i, carry):
            k0 = pl.multiple_of(2 * i * bkf, bkf)
            k1 = pl.multiple_of(k0 + bkf, bkf)
            k2 = pl.multiple_of(k1 + bkf, bkf)
            for hh in heads:
                st_b[hh] = scores_rel(k1, hh)
            carry = absorb([st_a[hh] for hh in heads], vp_ref, k0, bkf, rel_offs(k0), carry)
            for hh in heads:
                st_a[hh] = scores_rel(k2, hh)
            return absorb([st_b[hh] for hh in heads], vp_ref, k1, bkf, rel_offs(k1), carry)

        carry = lax.fori_loop(0, n_full // 2, body, carry)
        sd = [scores(k_ref, kd, bq, hh) + diag_scr[hh] for hh in heads]
        odd = (n_full & 1) == 1
        ko = pl.multiple_of(jnp.maximum(n_full - 1, 0) * bkf, bkf)
        pad = jnp.where(odd, 0.0, NO_KEYS)
        offs = [jnp.where(odd, o, 0.0) for o in rel_offs(ko)]
        carry = absorb([st_a[hh] + pad for hh in heads], vp_ref, ko, bkf, offs, carry)
    else:
        def body(j, carry):
            k0 = pl.multiple_of(j * bkf, bkf)
            return absorb([scores(kp_ref, k0, bkf, hh) + rel_scr[hh] for hh in heads], vp_ref, k0,
                          bkf, rel_offs(k0), carry)

        carry = lax.fori_loop(0, n_full, body, carry)
        sd = [scores(k_ref, kd, bq, hh) + diag_scr[hh] for hh in heads]
    carry = absorb(sd, v_ref, kd, bq, no_offs, carry)

    lam = scal_ref[layer, N_DIFF_HEADS]
    post = scal_ref[layer, N_DIFF_HEADS + 1]
    outs = []
    for hh in heads:
        on = acc_scr[hh] / carry[2 * hh + 1]
        ot = on[:, :bq] - lam * on[:, bq:]
        ot = ot * lax.rsqrt(jnp.mean(ot * ot, axis=0, keepdims=True) + SUBLN_EPS)
        outs.append((ot * sg_ref[...] * post).T)
    o = jnp.concatenate(outs, axis=1)
    o_ref[0] = (o * _silu(ga_ref[0].astype(F32))).astype(o_ref.dtype)


def _attn(qb, kb, vb, ga, prm, layer, bq, bkf, cache):
    b, tq, _ = qb.shape
    q_off = 0 if cache is None else cache[0].shape[1]
    assert q_off % bkf == 0 and (bq % bkf == 0 or tq == bq)
    assert q_off % CHUNK == 0 and bq % CHUNK == 0
    assert bq <= 256
    hs = HEADS_PER_STEP
    q_spec = pl.BlockSpec((1, bq, hs * PAIR), lambda i, h, qi: (i, qi, h))
    kv_spec = lambda tk: pl.BlockSpec((1, tk, hs * PAIR), lambda i, h, qi: (i, 0, h))
    past = [] if cache is None else list(cache)
    if bkf == bq:
        extra = [pltpu.VMEM((hs, bkf, PAIR), BF16), pltpu.VMEM((hs, 2 * bq, PAIR), BF16),
                 pltpu.VMEM((hs, bkf, 2 * bq), F32), pltpu.VMEM((hs, bkf, 2 * bq), F32)]
    else:
        extra = [pltpu.VMEM((hs, bkf, 2 * bq), F32)]
    kern = functools.partial(_attn_kernel, layer=layer, bq=bq, bkf=bkf, q_off=q_off,
                             has_cache=cache is not None)
    return pl.pallas_call(
        kern,
        grid=(b, N_DIFF_HEADS // hs, tq // bq),
        in_specs=[pl.BlockSpec(memory_space=pltpu.SMEM), q_spec, kv_spec(tq), kv_spec(tq)]
                 + [kv_spec(q_off)] * len(past)
                 + [q_spec, _layer_spec(prm["subln_col"], layer)],
        out_specs=q_spec,
        out_shape=jax.ShapeDtypeStruct((b, tq, D_DIFF), F32),
        scratch_shapes=[pltpu.VMEM((hs, PAIR, 2 * bq), F32), pltpu.VMEM((hs, bq, 2 * bq), F32)]
                       + extra,
        compiler_params=pltpu.CompilerParams(
            dimension_semantics=("arbitrary", "arbitrary", "arbitrary"),
            vmem_limit_bytes=VMEM_LIMIT),
        name="attn",
    )(prm["scal"], qb, kb, vb, *past, ga, prm["subln_col"])


def _merge_kernel(x_ref, or_ref, oa_ref, mg_ref, p_ref, wr_ref, wa_ref, wo_ref, pw_ref, pg_ref,
                  png_ref, y_ref):
    mg = mg_ref[...].astype(F32)
    u = (_sigmoid(mg[:, :D_MODEL]) * _dot(or_ref[...], wr_ref[...])
         + _sigmoid(mg[:, D_MODEL:]) * _dot(oa_ref[...], wa_ref[...]))
    x = x_ref[...] + _dot(u, wo_ref[...])
    e = _dot(p_ref[...], pw_ref[...])
    ms = jnp.mean(x * x, axis=-1, keepdims=True)
    hn = x * lax.rsqrt(ms + NORM_EPS) * png_ref[...]
    y_ref[...] = x + e * _sigmoid(_dot(hn, pg_ref[...]))


def _merge(x, o_r, o_a, mg, p_all, prm, layer, tm):
    n = x.shape[0]
    row = lambda w: pl.BlockSpec((tm, w), lambda i: (i, 0))
    p_spec = pl.BlockSpec((None, tm, PLE_DIM), lambda i: (layer, i, 0))
    ws = [prm["w_br_r"], prm["w_br_a"], prm["w_out"], prm["ple_w"], prm["ple_gate_w"],
          prm["ple_norm_g"]]
    return pl.pallas_call(
        _merge_kernel,
        grid=(n // tm,),
        in_specs=[row(D_MODEL), row(D_RWKV), row(D_DIFF), row(2 * D_MODEL), p_spec]
                 + [_layer_spec(w, layer) for w in ws],
        out_specs=row(D_MODEL),
        out_shape=jax.ShapeDtypeStruct((n, D_MODEL), F32),
        compiler_params=pltpu.CompilerParams(
            dimension_semantics=("parallel",), vmem_limit_bytes=VMEM_LIMIT),
        name="merge",
    )(x, o_r, o_a, mg, p_all, *ws)


def _prepare(norm_g, w_in, shift_mu, decay_w0, decay_w2, iclr_a0, iclr_a2, k_k, k_a, r_k, lnx_g,
             lnx_b, q_norm_g, k_norm_g, lambda_q1, lambda_k1, lambda_q2, lambda_k2, subln_g,
             w_br_r, w_br_a, w_out, ple_w, ple_gate_w, ple_norm_g):
    depth = w_in.shape[0]
    vec = lambda u: u.reshape(depth, 1, -1)
    zeros = jnp.zeros((depth, LORA, D_RWKV), F32)
    w_lora = jnp.concatenate([jnp.concatenate([decay_w2, zeros], axis=2),
                              jnp.concatenate([zeros, iclr_a2], axis=2)], axis=1).astype(BF16)
    lam_init = jnp.asarray([0.8 - 0.6 * math.exp(-0.3 * i) for i in range(depth)], F32)
    lam = (jnp.exp(jnp.sum(lambda_q1 * lambda_k1, axis=-1))
           - jnp.exp(jnp.sum(lambda_q2 * lambda_k2, axis=-1)) + lam_init)
    slopes = jnp.asarray([2.0 ** (-8.0 * (h + 1) / N_DIFF_HEADS) for h in range(N_DIFF_HEADS)], F32)
    scal = jnp.concatenate([jnp.broadcast_to(slopes, (depth, N_DIFF_HEADS)), lam[:, None],
                            (1.0 - lam_init)[:, None]], axis=1).astype(F32)
    hid = jnp.arange(HEAD_SUM_SLAB) // HEAD_DIM
    return dict(
        norm_g=vec(norm_g), w_in=w_in.astype(BF16),
        qg=vec(jnp.tile(q_norm_g, (1, 2 * N_DIFF_HEADS))), kg=vec(jnp.tile(k_norm_g, (1, 2 * N_DIFF_HEADS))),
        mu_main=vec(shift_mu[:, :SHIFT_MAIN]), mu_lora=vec(shift_mu[:, SHIFT_MAIN:]), w_lora=w_lora,
        w0=vec(decay_w0), a0=vec(iclr_a0), k_k=vec(k_k), k_a=vec(k_a), r_k=vec(r_k),
        lnx_g=vec(lnx_g), lnx_b=vec(lnx_b), scal=scal, subln_col=subln_g.reshape(depth, PAIR, 1),
        w_br_r=w_br_r.astype(BF16), w_br_a=w_br_a.astype(BF16), w_out=w_out.astype(BF16),
        ple_w=ple_w.astype(BF16), ple_gate_w=ple_gate_w.astype(BF16), ple_norm_g=vec(ple_norm_g),
        bd=(hid[:, None] == hid[None, :]).astype(BF16))


def _pair_state(s):
    b = s.shape[0]
    s = s.reshape(b, N_PAIRS, 2, HEAD_DIM, HEAD_DIM)
    z = jnp.zeros_like(s[:, :, 0])
    top = jnp.concatenate([s[:, :, 0], z], axis=-1)
    bot = jnp.concatenate([z, s[:, :, 1]], axis=-1)
    return jnp.concatenate([top, bot], axis=-2)


def _unpair_state(sp):
    b = sp.shape[0]
    h0 = sp[:, :, :HEAD_DIM, :HEAD_DIM]
    h1 = sp[:, :, HEAD_DIM:, HEAD_DIM:]
    return jnp.stack([h0, h1], axis=2).reshape(b, 2 * N_PAIRS, HEAD_DIM, HEAD_DIM)


def _group_layer(x, p_all, prm, layer, shift_prev, wkv_prev, cache):
    b, t, _ = x.shape
    n = b * t
    tl = _tiles(t)
    zs, zl, gr, ga, mg, k, v, qb, kb, vb = _in_proj(x.reshape(n, D_MODEL), prm, layer, tl["tm"])
    seq = lambda u: u.reshape(b, t, u.shape[-1])
    zs, zl = seq(zs), seq(zl)
    o_r, s_new = _rwkv(zs, zl, seq(gr), shift_prev[:, None, :SHIFT_MAIN],
                       shift_prev[:, None, SHIFT_MAIN:], _pair_state(wkv_prev), prm, layer,
                       tl["tb"])
    if cache is not None:
        cache = tuple(u.reshape(b, u.shape[1], D_DIFF) for u in cache)
    o_a = _attn(seq(qb), seq(kb), seq(vb), seq(ga), prm, layer, tl["bq"], tl["bkf"], cache)
    y = _merge(x.reshape(n, D_MODEL), o_r.reshape(n, D_RWKV), o_a.reshape(n, D_DIFF), mg,
               p_all.reshape(p_all.shape[0], n, PLE_DIM), prm, layer, tl["tm"])
    shift_new = jnp.concatenate([zs[:, -1], zl[:, -1]], axis=-1)
    return (y.reshape(b, t, D_MODEL), k.reshape(b, t, N_DIFF_HEADS, 2, HEAD_DIM),
            v.reshape(b, t, N_DIFF_HEADS, 2 * HEAD_DIM), _unpair_state(s_new), shift_new)


def kernel(x_prompt, x_sample, p_prompt, p_sample, cache_k, cache_v, state_wkv, state_shift, norm_g, w_in, shift_mu, decay_w0, decay_w2, iclr_a0, iclr_a2, k_k, k_a, r_k, lnx_g, lnx_b, q_norm_g, k_norm_g, lambda_q1, lambda_k1, lambda_q2, lambda_k2, subln_g, w_br_r, w_br_a, w_out, ple_w, ple_gate_w, ple_norm_g):
    depth = w_in.shape[0]
    b_p = x_prompt.shape[0]
    prm = _prepare(norm_g, w_in, shift_mu, decay_w0, decay_w2, iclr_a0, iclr_a2, k_k, k_a, r_k,
                   lnx_g, lnx_b, q_norm_g, k_norm_g, lambda_q1, lambda_k1, lambda_q2, lambda_k2,
                   subln_g, w_br_r, w_br_a, w_out, ple_w, ple_gate_w, ple_norm_g)
    zero_shift = jnp.zeros((b_p, SHIFT_MAIN + 2 * LORA), F32)
    zero_wkv = jnp.zeros((b_p, 2 * N_PAIRS, HEAD_DIM, HEAD_DIM), F32)
    y_p, y_s = x_prompt, x_sample
    outs_p, outs_s = [], []
    for i in range(depth):
        y_p, *rest_p = _group_layer(y_p, p_prompt, prm, i, zero_shift, zero_wkv, None)
        y_s, *rest_s = _group_layer(y_s, p_sample, prm, i, state_shift[i], state_wkv[i],
                                    (cache_k[i], cache_v[i]))
        outs_p.append(rest_p)
        outs_s.append(rest_s)
    stack = lambda outs, j: jnp.stack([o[j] for o in outs])
    return (y_p, y_s, stack(outs_p, 0), stack(outs_p, 1), stack(outs_p, 2), stack(outs_p, 3),
            stack(outs_s, 0), stack(outs_s, 1), stack(outs_s, 2), stack(outs_s, 3))
```

```python
import functools
import math

import jax
import jax.numpy as jnp
from jax import lax
from jax.experimental import pallas as pl
from jax.experimental.pallas import tpu as pltpu

F32 = jnp.float32
BF16 = jnp.bfloat16

D_MODEL = 1024
HEAD_DIM = 64
D_RWKV = 512
N_PAIRS = D_RWKV // (2 * HEAD_DIM)
PAIR = 2 * HEAD_DIM
LORA = 64
N_DIFF_HEADS = 4
D_DIFF = 512
PLE_DIM = 256
CHUNK = 64
NORM_EPS = 1e-6
GN_EPS = 64e-5
SUBLN_EPS = 1e-5
NEG_INF = -1e30
NO_KEYS = 2 * NEG_INF
SHIFT_MAIN = 3 * D_RWKV
C_ZS, C_ZL, C_GR, C_Q, C_K, C_V, C_GA, C_MG, C_END = (
    0, 1536, 1664, 2176, 2688, 3200, 3712, 4224, 6272)

HEADS_PER_STEP = 2
LOG2E = 1.4426950408889634
PREP_GROUP = 16
HEAD_SUM_SLAB = 256
HEAD_SUM_TERMS = 1
BIAS_LANES = 3
VMEM_LIMIT = 56 * 1024 * 1024


def _tiles(t):
    return dict(tm=512 if t >= 512 else 256, bq=min(256, t), bkf=256, tb=min(256, t))


def _dot(a, b):
    return jnp.dot(a.astype(BF16), b.astype(BF16), preferred_element_type=F32)


def _dot_nt(a, b):
    return lax.dot_general(a.astype(BF16), b.astype(BF16), (((1,), (1,)), ((), ())),
                           preferred_element_type=F32)


def _dot_tn(a, b):
    return lax.dot_general(a.astype(BF16), b.astype(BF16), (((0,), (0,)), ((), ())),
                           preferred_element_type=F32)


def _head_sums(u, bd, terms):
    w = bd.shape[0]
    return jnp.concatenate([_dot_exact_rhs(u[:, i:i + w], bd, terms)
                            for i in range(0, u.shape[1], w)], axis=1)


def _split_bf16(a, terms):
    parts = []
    rem = a
    for _ in range(terms):
        part = rem.astype(BF16)
        parts.append(part)
        rem = rem - part.astype(F32)
    return parts


def _dot_exact_rhs(a, b_exact, terms):
    return sum(jnp.dot(p, b_exact, preferred_element_type=F32) for p in _split_bf16(a, terms))


def _dot_exact_lhs(a_exact, b, terms):
    return sum(jnp.dot(a_exact, p, preferred_element_type=F32) for p in _split_bf16(b, terms))


def _sigmoid(x):
    return 1.0 / (1.0 + jnp.exp(-x))


def _silu(x):
    return x * _sigmoid(x)


def _const_spec(shape):
    nd = len(shape)
    return pl.BlockSpec(shape, lambda *_: (0,) * nd)


def _layer_spec(arr, layer):
    nd = arr.ndim - 1
    return pl.BlockSpec((None,) + arr.shape[1:], lambda *_: (layer,) + (0,) * nd)


def _in_proj_kernel(x_ref, g_ref, w_ref, qg_ref, kg_ref, bd_ref,
                    zs_ref, zl_ref, gr_ref, ga_ref, mg_ref, k_ref, v_ref,
                    qb_ref, kb_ref, vb_ref):
    x = x_ref[...]
    ms = jnp.mean(x * x, axis=-1, keepdims=True)
    h = (x * lax.rsqrt(ms + NORM_EPS) * g_ref[...]).astype(BF16)

    def proj(lo, hi):
        return jnp.dot(h, w_ref[:, lo:hi], preferred_element_type=F32)

    zs_ref[...] = proj(C_ZS, C_ZL)
    zl_ref[...] = proj(C_ZL, C_GR)
    gr_ref[...] = proj(C_GR, C_Q).astype(BF16)
    ga_ref[...] = proj(C_GA, C_MG).astype(BF16)
    mg_ref[...] = proj(C_MG, C_END).astype(BF16)
    v = proj(C_V, C_GA)
    v_ref[...] = v
    vb_ref[...] = v.astype(BF16)

    bd = bd_ref[...]

    def head_norm(t, gain):
        ss = _head_sums(t * t, bd, HEAD_SUM_TERMS)
        return t * lax.rsqrt(ss * (1.0 / HEAD_DIM) + NORM_EPS) * gain

    q = head_norm(proj(C_Q, C_K), qg_ref[...])
    qb_ref[...] = (q * (LOG2E * HEAD_DIM ** -0.5)).astype(BF16)
    k = head_norm(proj(C_K, C_V), kg_ref[...])
    k_ref[...] = k
    kb_ref[...] = k.astype(BF16)


def _in_proj(x, prm, layer, tm):
    n = x.shape[0]
    row = lambda w: pl.BlockSpec((tm, w), lambda i: (i, 0))
    out_shapes = [
        jax.ShapeDtypeStruct((n, SHIFT_MAIN), F32),
        jax.ShapeDtypeStruct((n, 2 * LORA), F32),
        jax.ShapeDtypeStruct((n, D_RWKV), BF16),
        jax.ShapeDtypeStruct((n, D_DIFF), BF16),
        jax.ShapeDtypeStruct((n, 2 * D_MODEL), BF16),
        jax.ShapeDtypeStruct((n, D_DIFF), F32),
        jax.ShapeDtypeStruct((n, D_DIFF), F32),
        jax.ShapeDtypeStruct((n, D_DIFF), BF16),
        jax.ShapeDtypeStruct((n, D_DIFF), BF16),
        jax.ShapeDtypeStruct((n, D_DIFF), BF16),
    ]
    params = [prm["norm_g"], prm["w_in"], prm["qg"], prm["kg"]]
    return pl.pallas_call(
        _in_proj_kernel,
        grid=(n // tm,),
        in_specs=[row(D_MODEL)] + [_layer_spec(p, layer) for p in params]
                 + [_const_spec(prm["bd"].shape)],
        out_specs=[row(s.shape[1]) for s in out_shapes],
        out_shape=out_shapes,
        compiler_params=pltpu.CompilerParams(
            dimension_semantics=("parallel",), vmem_limit_bytes=VMEM_LIMIT),
        name="in_proj",
    )(x, *params, prm["bd"])


def _rwkv_kernel(zs_ref, zl_ref, gr_ref, spm_ref, spl_ref, s0_ref,
                 mum_ref, mul_ref, wl_ref, w0_ref, a0_ref, kk_ref, ka_ref, rk_ref,
                 lg_ref, lb_ref, bd_ref,
                 o_ref, so_ref,
                 cm_scr, cl_scr, s_scr, *, tb):
    t = pl.program_id(1)
    c = CHUNK
    n_chunks = tb // c

    @pl.when(t == 0)
    def _():
        cm_scr[...] = spm_ref[0]
        cl_scr[...] = spl_ref[0]
        s_scr[...] = s0_ref[0]

    row = lax.broadcasted_iota(jnp.int32, (tb, 1), 0)

    def shifted(z_ref, carry, mu_ref):
        z = z_ref[0]
        zp = jnp.where(row == 0, carry[...], pltpu.roll(z, 1, axis=0))
        carry[...] = z[tb - 1:tb, :]
        return z + (zp - z) * mu_ref[...]

    zm = shifted(zs_ref, cm_scr, mum_ref)
    zl = shifted(zl_ref, cl_scr, mul_ref)
    r_ = zm[:, :D_RWKV]
    k_ = zm[:, D_RWKV:2 * D_RWKV]
    v_ = zm[:, 2 * D_RWKV:]

    lane = lax.broadcasted_iota(jnp.int32, (tb, 2 * LORA), 1)
    lin = jnp.where(lane < LORA, jnp.tanh(zl), zl)
    lo = _dot(lin, wl_ref[...])
    wpre = w0_ref[...] + lo[:, :D_RWKV]
    apre = a0_ref[...] + lo[:, D_RWKV:]
    sp = jnp.maximum(-wpre, 0.0) + jnp.log(1.0 + jnp.exp(-jnp.abs(wpre)))
    lw = -jnp.exp(-sp - 0.5)
    a = _sigmoid(apre)

    bd = bd_ref[...]

    def head_sum(u):
        return _head_sums(u, bd, HEAD_SUM_TERMS)

    kkr = k_ * kk_ref[...]
    kk = kkr / jnp.maximum(jnp.sqrt(head_sum(kkr * kkr)), 1e-12)
    kf = k_ * (1.0 + (a - 1.0) * ka_ref[...])

    ri = lax.broadcasted_iota(jnp.int32, (tb, tb), 0)
    ci = lax.broadcasted_iota(jnp.int32, (tb, tb), 1)
    same = (ri >> 6) == (ci >> 6)
    tri = jnp.where(jnp.logical_and(same, ci <= ri), 1.0, 0.0).astype(BF16)
    cs = _dot_exact_lhs(tri, lw, 3)
    if n_chunks == 1:
        ctot = cs[c - 1:c, :]
    else:
        ctot = _dot_exact_lhs(jnp.where(same, 1.0, 0.0).astype(BF16), lw, 3)
    g = jnp.exp(cs)
    ginv = jnp.exp(-cs)
    gend = jnp.exp(ctot - cs)
    ka = kk * a
    a_t = -kk * jnp.exp(cs - lw)
    b_t = ka * ginv
    k_t = kf * ginv
    r_t = r_ * g
    b_h = ka * gend
    k_h = kf * gend

    lane_p = lax.broadcasted_iota(jnp.int32, (c, PAIR), 1)
    first = lane_p < HEAD_DIM

    def stack(u):
        return jnp.concatenate([jnp.where(first, u, 0.0), jnp.where(first, 0.0, u)], axis=0)

    r2 = lax.broadcasted_iota(jnp.int32, (2 * c, 2 * c), 0)
    c2 = lax.broadcasted_iota(jnp.int32, (2 * c, 2 * c), 1)
    strict = c2 < r2
    eye = jnp.where(c2 == r2, 1.0, 0.0)
    r4 =lax.broadcasted_iota(jnp.int32, (2 * c, 4 * c), 0)
    c4 = lax.broadcasted_iota(jnp.int32, (2 * c, 4 * c), 1)
    incl = jnp.where(c4 >= 2 * c, c4 - 2 * c, c4) <= r4

    def prepare(blocks):
        def gather(u):
            return [stack(u[ch * c:(ch + 1) * c, p * PAIR:(p + 1) * PAIR]).astype(BF16)
                    for ch, p in blocks]

        a_s, r_s, b_s, k_s, v_s = gather(a_t), gather(r_t), gather(b_t), gather(k_t), gather(v_)
        bkh = [jnp.concatenate([x, y], axis=0) for x, y in zip(gather(b_h), gather(k_h))]
        gmat = [_dot_nt(jnp.concatenate([x, y], axis=0), jnp.concatenate([z, w], axis=0))
                for x, y, z, w in zip(a_s, r_s, b_s, k_s)]
        a_ak = [jnp.where(strict, gm[:2 * c, 2 * c:], 0.0) for gm in gmat]
        a_r = [jnp.where(incl, gm[2 * c:, :], 0.0).astype(BF16) for gm in gmat]
        lp = [jnp.where(strict, gm[:2 * c, :2 * c], 0.0) for gm in gmat]
        inv = [eye + x for x in lp]
        for _ in range(5):
            lp = [_dot(x, x) for x in lp]
            inv = [y + _dot(y, x) for x, y in zip(lp, inv)]
        w_s = [_dot(y, x).astype(BF16) for x, y in zip(a_s, inv)]
        x_s = [_dot(x, y) for x, y in zip(a_ak, v_s)]
        x_s = [_dot(y, x) for x, y in zip(x_s, inv)]
        return w_s, x_s, a_r, r_s, v_s, bkh

    blocks = [(ch, p) for ch in range(n_chunks) for p in range(N_PAIRS)]
    groups = [prepare(blocks[i:i + PREP_GROUP]) for i in range(0, len(blocks), PREP_GROUP)]
    w_s, x_s, a_r, r_s, v_s, bkh = (sum(parts, []) for parts in zip(*groups))

    outs = []
    pairs = range(N_PAIRS)
    s = [s_scr[p] for p in pairs]
    for ch in range(n_chunks):
        glast = jnp.exp(ctot[ch * c:ch * c + 1, :])
        at = lambda lst: lst[ch * N_PAIRS:(ch + 1) * N_PAIRS]
        u_s = [_dot_nt(w, sp) + x for w, sp, x in zip(at(w_s), s, at(x_s))]
        uv = [jnp.concatenate([u.astype(BF16), v], axis=0) for u, v in zip(u_s, at(v_s))]
        rs = [_dot_nt(r, sp) for r, sp in zip(at(r_s), s)]
        o_s = [x + _dot(ar, u) for x, ar, u in zip(rs, at(a_r), uv)]
        upd = [_dot_tn(u, bk) for u, bk in zip(uv, at(bkh))]
        s = [s[p] * glast[:, p * PAIR:(p + 1) * PAIR] + upd[p] for p in pairs]
        outs.append(jnp.concatenate([x[:c] + x[c:] for x in o_s], axis=1))
    for p in pairs:
        s_scr[p] = s[p]
    o = jnp.concatenate(outs, axis=0)

    mu = head_sum(o) * (1.0 / HEAD_DIM)
    d = o - mu
    var = head_sum(d * d) * (1.0 / HEAD_DIM)
    bonus = head_sum(r_ * kf * rk_ref[...]) * v_
    yn = d * lax.rsqrt(var + GN_EPS) * lg_ref[...] + lb_ref[...] + bonus
    o_ref[0] = (yn * _silu(gr_ref[0].astype(F32))).astype(o_ref.dtype)

    @pl.when(t == pl.num_programs(1) - 1)
    def _():
        so_ref[0] = s_scr[...]


def _rwkv(zs, zl, gr, spm, spl, s0, prm, layer, tb):
    b, t, _ = zs.shape
    seq = lambda w: pl.BlockSpec((1, tb, w), lambda i, j: (i, j, 0))
    per_b = lambda shp: pl.BlockSpec((1,) + shp, lambda i, j: (i,) + (0,) * len(shp))
    params = [prm["mu_main"], prm["mu_lora"], prm["w_lora"], prm["w0"], prm["a0"], prm["k_k"],
              prm["k_a"], prm["r_k"], prm["lnx_g"], prm["lnx_b"]]
    return pl.pallas_call(
        functools.partial(_rwkv_kernel, tb=tb),
        grid=(b, t // tb),
        in_specs=[seq(SHIFT_MAIN), seq(2 * LORA), seq(D_RWKV),
                  per_b((1, SHIFT_MAIN)), per_b((1, 2 * LORA)), per_b((N_PAIRS, PAIR, PAIR))]
                 + [_layer_spec(p, layer) for p in params] + [_const_spec(prm["bd"].shape)],
        out_specs=[seq(D_RWKV), per_b((N_PAIRS, PAIR, PAIR))],
        out_shape=[jax.ShapeDtypeStruct((b, t, D_RWKV), F32),
                   jax.ShapeDtypeStruct((b, N_PAIRS, PAIR, PAIR), F32)],
        scratch_shapes=[pltpu.VMEM((1, SHIFT_MAIN), F32), pltpu.VMEM((1, 2 * LORA), F32),
                        pltpu.VMEM((N_PAIRS, PAIR, PAIR), F32)],
        compiler_params=pltpu.CompilerParams(
            dimension_semantics=("parallel", "arbitrary"), vmem_limit_bytes=VMEM_LIMIT),
        name="rwkv",
    )(zs, zl, gr, spm, spl, s0, *params, prm["bd"])


def _attn_kernel(scal_ref, q_ref, k_ref, v_ref, *refs, layer, bq, bkf, q_off, has_cache):
    if has_cache:
        kp_ref, vp_ref, *refs = refs
    else:
        kp_ref, vp_ref = k_ref, v_ref
    ga_ref, sg_ref, o_ref, acc_scr, diag_scr, *scr = refs
    hp = pl.program_id(1)
    qi = pl.program_id(2)
    heads = range(HEADS_PER_STEP)
    lanes = lambda hh: slice(hh * PAIR, (hh + 1) * PAIR)
    slope = [scal_ref[layer, HEADS_PER_STEP * hp + hh] * LOG2E for hh in heads]
    folded = bkf == bq
    if folded:
        kx_scr, qx_scr, st_a, st_b = scr
    else:
        rel_scr, = scr

    @pl.when(qi == 0)
    def _():
        i_d = lax.broadcasted_iota(jnp.int32, (bq, 2 * bq), 1)
        i_d = jnp.where(i_d >= bq, i_d - bq, i_d)
        j_d = lax.broadcasted_iota(jnp.int32, (bq, 2 * bq), 0)
        dist = jnp.abs(i_d - j_d).astype(F32)
        allowed = (j_d >> 6) <= (i_d >> 6)
        for hh in heads:
            diag_scr[hh] = jnp.where(allowed, -slope[hh] * dist, NEG_INF)
            if folded:
                def extra(rows, pos_first):
                    lane = lax.broadcasted_iota(jnp.int32, (rows, PAIR), 1)
                    pos = lax.broadcasted_iota(jnp.int32, (rows, PAIR), 0)
                    pos = jnp.where(pos >= bq, pos - bq, pos).astype(F32)
                    terms = _split_bf16(jnp.full((rows, PAIR), slope[hh], F32), BIAS_LANES)
                    term = terms[BIAS_LANES - 1].astype(F32)
                    for n in range(BIAS_LANES - 1):
                        here = jnp.logical_or(lane == n, lane == n + BIAS_LANES)
                        term = jnp.where(here, terms[n].astype(F32), term)
                    lo, hi = (pos, term) if pos_first else (term, -pos)
                    out = jnp.where(lane < BIAS_LANES, lo,
                                    jnp.where(lane < 2 * BIAS_LANES, hi, 0.0))
                    return out.astype(BF16)

                kx_scr[hh] = extra(bkf, True)
                qx_scr[hh] = extra(2 * bq, False)
            else:
                i_f =lax.broadcasted_iota(jnp.int32, (bkf, 2 * bq), 1)
                i_f = jnp.where(i_f >= bq, i_f - bq, i_f)
                j_f = lax.broadcasted_iota(jnp.int32, (bkf, 2 * bq), 0)
                rel_scr[hh] = (j_f - i_f).astype(F32) * slope[hh]

    first = lax.broadcasted_iota(jnp.int32, (bq, PAIR), 1) < HEAD_DIM
    qs = []
    for hh in heads:
        q = q_ref[0, :, lanes(hh)]
        zero = jnp.zeros_like(q)
        qs.append(jnp.concatenate([jnp.where(first, q, zero), jnp.where(first, zero, q)], axis=0))
    acc_scr[...] = jnp.zeros(acc_scr.shape, F32)
    q0 = q_off + qi * bq
    nt = (((1,), (1,)), ((), ()))

    def tile(ref, k0, size, hh):
        return ref[0, pl.ds(k0, size), lanes(hh)].astype(BF16)

    def scores(ref, k0, size, hh):
        return lax.dot_general(tile(ref, k0, size, hh), qs[hh], nt, preferred_element_type=F32)

    def absorb(sts, ref, k0, size, offs, carry):
        new, upd = [], []
        for hh in heads:
            m, l, st = carry[2 * hh], carry[2 * hh + 1], sts[hh]
            m_new = jnp.maximum(m, jnp.max(st, axis=0, keepdims=True) + offs[hh])
            p = jnp.exp2(st - (m_new - offs[hh]))
            alpha = jnp.exp2(m - m_new)
            new += [m_new, alpha * l + jnp.sum(p, axis=0, keepdims=True)]
            pv = lax.dot_general(tile(ref, k0, size, hh), p.astype(BF16),
                                 (((0,), (0,)), ((), ())), preferred_element_type=F32)
            upd.append((alpha, pv))
        for hh in heads:
            acc_scr[hh] = upd[hh][0] * acc_scr[hh] + upd[hh][1]
        return tuple(new)

    def rel_offs(k0):
        return [-slope[hh] * (q0 - k0).astype(F32) for hh in heads]

    carry = (jnp.full((1, 2 * bq), NEG_INF, F32), jnp.zeros((1, 2 * bq), F32)) * HEADS_PER_STEP
    n_full = q0 // bkf
    kd = pl.multiple_of(qi * bq + (0 if has_cache else q_off), bq)
    no_offs = [0.0] * HEADS_PER_STEP
    if folded:
        qa =[jnp.concatenate([qs[hh], qx_scr[hh]], axis=1) for hh in heads]

        def scores_rel(k0, hh):
            ka = jnp.concatenate([tile(kp_ref, k0, bkf, hh), kx_scr[hh]], axis=1)
            return lax.dot_general(ka, qa[hh], nt, preferred_element_type=F32)

        for hh in heads:
            st_a[hh] = scores_rel(0, hh)

        def body(i, carry):
            k0 = pl.multiple_of(2 * i * bkf, bkf)
            k1 = pl.multiple_of(k0 + bkf, bkf)
            k2 = pl.multiple_of(k1 + bkf, bkf)
            for hh in heads:
                st_b[hh] = scores_rel(k1, hh)
            carry = absorb([st_a[hh] for hh in heads], vp_ref, k0, bkf, rel_offs(k0), carry)
            for hh in heads:
                st_a[hh] = scores_rel(k2, hh)
            return absorb([st_b[hh] for hh in heads], vp_ref, k1, bkf, rel_offs(k1), carry)

        carry = lax.fori_loop(0, n_full // 2, body, carry)
        sd = [scores(k_ref, kd, bq, hh) + diag_scr[hh] for hh in heads]
        odd = (n_full & 1) == 1
        ko = pl.multiple_of(jnp.maximum(n_full - 1, 0) * bkf, bkf)
        pad = jnp.where(odd, 0.0, NO_KEYS)
        offs = [jnp.where(odd, o, 0.0) for o in rel_offs(ko)]
        carry = absorb([st_a[hh] + pad for hh in heads], vp_ref, ko, bkf, offs, carry)
    else:
        def body(j, carry):
            k0 = pl.multiple_of(j * bkf, bkf)
            return absorb([scores(kp_ref, k0, bkf, hh) + rel_scr[hh] for hh in heads], vp_ref, k0,
                          bkf, rel_offs(k0), carry)

        carry = lax.fori_loop(0, n_full, body, carry)
        sd = [scores(k_ref, kd, bq, hh) + diag_scr[hh] for hh in heads]
    carry = absorb(sd, v_ref, kd, bq, no_offs, carry)

    lam = scal_ref[layer, N_DIFF_HEADS]
    post = scal_ref[layer, N_DIFF_HEADS + 1]
    outs = []
    for hh in heads:
        on = acc_scr[hh] / carry[2 * hh + 1]
        ot = on[:, :bq] - lam * on[:, bq:]
        ot = ot * lax.rsqrt(jnp.mean(ot * ot, axis=0, keepdims=True) + SUBLN_EPS)
        outs.append((ot * sg_ref[...] * post).T)
    o = jnp.concatenate(outs, axis=1)
    o_ref[0] = (o * _silu(ga_ref[0].astype(F32))).astype(o_ref.dtype)


def _attn(qb, kb, vb, ga, prm, layer, bq, bkf, cache):
    b, tq, _ = qb.shape
    q_off = 0 if cache is None else cache[0].shape[2]
    assert q_off % bkf == 0 and (bq % bkf == 0 or tq == bq)
    assert q_off % CHUNK == 0 and bq % CHUNK == 0
    assert bq <= 256
    hs = HEADS_PER_STEP
    q_spec = pl.BlockSpec((1, bq, hs * PAIR), lambda i, h, qi: (i, qi, h))
    kv_spec = lambda tk: pl.BlockSpec((1, tk, hs * PAIR), lambda i, h, qi: (i, 0, h))
    past = [] if cache is None else list(cache)
    past_spec = pl.BlockSpec((None, 1, q_off, hs * PAIR), lambda i, h, qi: (layer, i, 0, h))
    if bkf == bq:
        extra = [pltpu.VMEM((hs, bkf, PAIR), BF16), pltpu.VMEM((hs, 2 * bq, PAIR), BF16),
                 pltpu.VMEM((hs, bkf, 2 * bq), F32), pltpu.VMEM((hs, bkf, 2 * bq), F32)]
    else:
        extra = [pltpu.VMEM((hs, bkf, 2 * bq), F32)]
    kern = functools.partial(_attn_kernel, layer=layer, bq=bq, bkf=bkf, q_off=q_off,
                             has_cache=cache is not None)
    return pl.pallas_call(
        kern,
        grid=(b, N_DIFF_HEADS // hs, tq // bq),
        in_specs=[pl.BlockSpec(memory_space=pltpu.SMEM), q_spec, kv_spec(tq), kv_spec(tq)]
                 + [past_spec] * len(past)
                 + [q_spec, _layer_spec(prm["subln_col"], layer)],
        out_specs=q_spec,
        out_shape=jax.ShapeDtypeStruct((b, tq, D_DIFF), F32),
        scratch_shapes=[pltpu.VMEM((hs, PAIR, 2 * bq), F32), pltpu.VMEM((hs, bq, 2 * bq), F32)]
                       + extra,
        compiler_params=pltpu.CompilerParams(
            dimension_semantics=("arbitrary", "arbitrary", "arbitrary"),
            vmem_limit_bytes=VMEM_LIMIT),
        name="attn",
    )(prm["scal"], qb, kb, vb, *past, ga, prm["subln_col"])


def _merge_kernel(x_ref, or_ref, oa_ref, mg_ref, p_ref, wr_ref, wa_ref, wo_ref, pw_ref, pg_ref,
                  png_ref, y_ref):
    mg = mg_ref[...].astype(F32)
    u = (_sigmoid(mg[:, :D_MODEL]) * _dot(or_ref[...], wr_ref[...])
         + _sigmoid(mg[:, D_MODEL:]) * _dot(oa_ref[...], wa_ref[...]))
    x = x_ref[...] + _dot(u, wo_ref[...])
    e = _dot(p_ref[...], pw_ref[...])
    ms = jnp.mean(x * x, axis=-1, keepdims=True)
    hn = x * lax.rsqrt(ms + NORM_EPS) * png_ref[...]
    y_ref[...] = x + e * _sigmoid(_dot(hn, pg_ref[...]))


def _merge(x, o_r, o_a, mg, p_all, prm, layer, tm):
    n = x.shape[0]
    row = lambda w: pl.BlockSpec((tm, w), lambda i: (i, 0))
    p_spec = pl.BlockSpec((None, tm, PLE_DIM), lambda i: (layer, i, 0))
    ws = [prm["w_br_r"], prm["w_br_a"], prm["w_out"], prm["ple_w"], prm["ple_gate_w"],
          prm["ple_norm_g"]]
    return pl.pallas_call(
        _merge_kernel,
        grid=(n // tm,),
        in_specs=[row(D_MODEL), row(D_RWKV), row(D_DIFF), row(2 * D_MODEL), p_spec]
                 + [_layer_spec(w, layer) for w in ws],
        out_specs=row(D_MODEL),
        out_shape=jax.ShapeDtypeStruct((n, D_MODEL), F32),
        compiler_params=pltpu.CompilerParams(
            dimension_semantics=("parallel",), vmem_limit_bytes=VMEM_LIMIT),
        name="merge",
    )(x, o_r, o_a, mg, p_all, *ws)


def _prepare(norm_g, w_in, shift_mu, decay_w0, decay_w2, iclr_a0, iclr_a2, k_k, k_a, r_k, lnx_g,
             lnx_b, q_norm_g, k_norm_g, lambda_q1, lambda_k1, lambda_q2, lambda_k2, subln_g,
             w_br_r, w_br_a, w_out, ple_w, ple_gate_w, ple_norm_g):
    depth = w_in.shape[0]
    vec = lambda u: u.reshape(depth, 1, -1)
    zeros = jnp.zeros((depth, LORA, D_RWKV), F32)
    w_lora = jnp.concatenate([jnp.concatenate([decay_w2, zeros], axis=2),
                              jnp.concatenate([zeros, iclr_a2], axis=2)], axis=1).astype(BF16)
    lam_init = jnp.asarray([0.8 - 0.6 * math.exp(-0.3 * i) for i in range(depth)], F32)
    lam = (jnp.exp(jnp.sum(lambda_q1 * lambda_k1, axis=-1))
           - jnp.exp(jnp.sum(lambda_q2 * lambda_k2, axis=-1)) + lam_init)
    slopes = jnp.asarray([2.0 ** (-8.0 * (h + 1) / N_DIFF_HEADS) for h in range(N_DIFF_HEADS)], F32)
    scal = jnp.concatenate([jnp.broadcast_to(slopes, (depth, N_DIFF_HEADS)), lam[:, None],
                            (1.0 - lam_init)[:, None]], axis=1).astype(F32)
    hid = jnp.arange(HEAD_SUM_SLAB) // HEAD_DIM
    return dict(
        norm_g=vec(norm_g), w_in=w_in.astype(BF16),
        qg=vec(jnp.tile(q_norm_g, (1, 2 * N_DIFF_HEADS))), kg=vec(jnp.tile(k_norm_g, (1, 2 * N_DIFF_HEADS))),
        mu_main=vec(shift_mu[:, :SHIFT_MAIN]), mu_lora=vec(shift_mu[:, SHIFT_MAIN:]), w_lora=w_lora,
        w0=vec(decay_w0), a0=vec(iclr_a0), k_k=vec(k_k), k_a=vec(k_a), r_k=vec(r_k),
        lnx_g=vec(lnx_g), lnx_b=vec(lnx_b), scal=scal, subln_col=subln_g.reshape(depth, PAIR, 1),
        w_br_r=w_br_r.astype(BF16), w_br_a=w_br_a.astype(BF16), w_out=w_out.astype(BF16),
        ple_w=ple_w.astype(BF16), ple_gate_w=ple_gate_w.astype(BF16), ple_norm_g=vec(ple_norm_g),
        bd=(hid[:, None] == hid[None, :]).astype(BF16))


def _pair_state(s):
    b = s.shape[0]
    s = s.reshape(b, N_PAIRS, 2, HEAD_DIM, HEAD_DIM)
    z = jnp.zeros_like(s[:, :, 0])
    top = jnp.concatenate([s[:, :, 0], z], axis=-1)
    bot = jnp.concatenate([z, s[:, :, 1]], axis=-1)
    return jnp.concatenate([top, bot], axis=-2)


def _unpair_state(sp):
    b = sp.shape[0]
    h0 = sp[:, :, :HEAD_DIM, :HEAD_DIM]
    h1 = sp[:, :, HEAD_DIM:, HEAD_DIM:]
    return jnp.stack([h0, h1], axis=2).reshape(b, 2 * N_PAIRS, HEAD_DIM, HEAD_DIM)


def _group_layer(x, p_all, prm, layer, shift_prev, wkv_prev, cache):
    b, t, _ = x.shape
    n = b * t
    tl = _tiles(t)
    zs, zl, gr, ga, mg, k, v, qb, kb, vb = _in_proj(x.reshape(n, D_MODEL), prm, layer, tl["tm"])
    seq = lambda u: u.reshape(b, t, u.shape[-1])
    zs, zl = seq(zs), seq(zl)
    o_r, s_new = _rwkv(zs, zl, seq(gr), shift_prev[:, None, :SHIFT_MAIN],
                       shift_prev[:, None, SHIFT_MAIN:], _pair_state(wkv_prev), prm, layer,
                       tl["tb"])
    o_a = _attn(seq(qb), seq(kb), seq(vb), seq(ga), prm, layer, tl["bq"], tl["bkf"], cache)
    y = _merge(x.reshape(n, D_MODEL), o_r.reshape(n, D_RWKV), o_a.reshape(n, D_DIFF), mg,
               p_all.reshape(p_all.shape[0], n, PLE_DIM), prm, layer, tl["tm"])
    shift_new = jnp.concatenate([zs[:, -1], zl[:, -1]], axis=-1)
    return (y.reshape(b, t, D_MODEL), k.reshape(b, t, N_DIFF_HEADS, 2, HEAD_DIM),
            v.reshape(b, t, N_DIFF_HEADS, 2 * HEAD_DIM), _unpair_state(s_new), shift_new)


def kernel(x_prompt, x_sample, p_prompt, p_sample, cache_k, cache_v, state_wkv, state_shift, norm_g, w_in, shift_mu, decay_w0, decay_w2, iclr_a0, iclr_a2, k_k, k_a, r_k, lnx_g, lnx_b, q_norm_g, k_norm_g, lambda_q1, lambda_k1, lambda_q2, lambda_k2, subln_g, w_br_r, w_br_a, w_out, ple_w, ple_gate_w, ple_norm_g):
    depth = w_in.shape[0]
    b_p = x_prompt.shape[0]
    prm = _prepare(norm_g, w_in, shift_mu, decay_w0, decay_w2, iclr_a0, iclr_a2, k_k, k_a, r_k,
                   lnx_g, lnx_b, q_norm_g, k_norm_g, lambda_q1, lambda_k1, lambda_q2, lambda_k2,
                   subln_g, w_br_r, w_br_a, w_out, ple_w, ple_gate_w, ple_norm_g)
    zero_shift = jnp.zeros((b_p, SHIFT_MAIN + 2 * LORA), F32)
    zero_wkv = jnp.zeros((b_p, 2 * N_PAIRS, HEAD_DIM, HEAD_DIM), F32)
    cache = tuple(u.reshape(u.shape[:3] + (D_DIFF,)) for u in (cache_k, cache_v))
    y_p, y_s = x_prompt, x_sample
    outs_p, outs_s = [], []
    for i in range(depth):
        y_p, *rest_p = _group_layer(y_p, p_prompt, prm, i, zero_shift, zero_wkv, None)
        y_s, *rest_s = _group_layer(y_s, p_sample, prm, i, state_shift[i], state_wkv[i],
                                    cache)
        outs_p.append(rest_p)
        outs_s.append(rest_s)
    stack = lambda outs, j: jnp.stack([o[j] for o in outs])
    return (y_p, y_s, stack(outs_p, 0), stack(outs_p, 1), stack(outs_p, 2), stack(outs_p, 3),
            stack(outs_s, 0), stack(outs_s, 1), stack(outs_s, 2), stack(outs_s, 3))
```

```python
import functools
import math

import jax
import jax.numpy as jnp
from jax import lax
from jax.experimental import pallas as pl
from jax.experimental.pallas import tpu as pltpu

F32 = jnp.float32
BF16 = jnp.bfloat16

D_MODEL = 1024
HEAD_DIM = 64
D_RWKV = 512
N_PAIRS = D_RWKV // (2 * HEAD_DIM)
PAIR = 2 * HEAD_DIM
LORA = 64
N_DIFF_HEADS = 4
D_DIFF = 512
PLE_DIM = 256
CHUNK = 64
NORM_EPS = 1e-6
GN_EPS = 64e-5
SUBLN_EPS = 1e-5
NEG_INF = -1e30
NO_KEYS = 2 * NEG_INF
SHIFT_MAIN = 3 * D_RWKV
C_ZS, C_ZL, C_GR, C_Q, C_K, C_V, C_GA, C_MG, C_END = (
    0, 1536, 1664, 2176, 2688, 3200, 3712, 4224, 6272)

HEADS_PER_STEP = 2
LOG2E = 1.4426950408889634
PREP_GROUP = 16
HEAD_SUM_SLAB = 256
HEAD_SUM_TERMS = 1
BIAS_LANES = 3
VMEM_LIMIT = 56 * 1024 * 1024


def _tiles(t):
    return dict(tm=512 if t >= 512 else 256, bq=min(256, t), bkf=256, tb=min(256, t))


def _dot(a, b):
    return jnp.dot(a.astype(BF16), b.astype(BF16), preferred_element_type=F32)


def _dot_nt(a, b):
    return lax.dot_general(a.astype(BF16), b.astype(BF16), (((1,), (1,)), ((), ())),
                           preferred_element_type=F32)


def _dot_tn(a, b):
    return lax.dot_general(a.astype(BF16), b.astype(BF16), (((0,), (0,)), ((), ())),
                           preferred_element_type=F32)


def _head_sums(u, bd, terms):
    w = bd.shape[0]
    return jnp.concatenate([_dot_exact_rhs(u[:, i:i + w], bd, terms)
                            for i in range(0, u.shape[1], w)], axis=1)


def _split_bf16(a, terms):
    parts = []
    rem = a
    for _ in range(terms):
        part = rem.astype(BF16)
        parts.append(part)
        rem = rem - part.astype(F32)
    return parts


def _dot_exact_rhs(a, b_exact, terms):
    return sum(jnp.dot(p, b_exact, preferred_element_type=F32) for p in _split_bf16(a, terms))


def _dot_exact_lhs(a_exact, b, terms):
    return sum(jnp.dot(a_exact, p, preferred_element_type=F32) for p in _split_bf16(b, terms))


def _sigmoid(x):
    return 1.0 / (1.0 + jnp.exp(-x))


def _silu(x):
    return x * _sigmoid(x)


def _const_spec(shape):
    nd = len(shape)
    return pl.BlockSpec(shape, lambda *_: (0,) * nd)


def _layer_spec(arr, layer):
    nd = arr.ndim - 1
    return pl.BlockSpec((None,) + arr.shape[1:], lambda *_: (layer,) + (0,) * nd)


def _in_proj_kernel(x_ref, g_ref, w_ref, qg_ref, kg_ref, bd_ref, *refs):
    zs_ref, zl_ref, gr_ref, ga_ref, mg_ref, k_ref, v_ref, qb_ref, kb_ref, vb_ref = refs[-10:]
    x = x_ref[...]
    ms = jnp.mean(x * x, axis=-1, keepdims=True)
    h = (x * lax.rsqrt(ms + NORM_EPS) * g_ref[...]).astype(BF16)

    def proj(lo, hi):
        return jnp.dot(h, w_ref[:, lo:hi], preferred_element_type=F32)

    zs_ref[...] = proj(C_ZS, C_ZL)
    zl_ref[...] = proj(C_ZL, C_GR)
    gr_ref[...] = proj(C_GR, C_Q).astype(BF16)
    ga_ref[...] = proj(C_GA, C_MG).astype(BF16)
    mg_ref[...] = proj(C_MG, C_END).astype(BF16)
    v = proj(C_V, C_GA)
    v_ref[...] = v
    vb_ref[...] = v.astype(BF16)

    bd = bd_ref[...]

    def head_norm(t, gain):
        ss = _head_sums(t * t, bd, HEAD_SUM_TERMS)
        return t * lax.rsqrt(ss * (1.0 / HEAD_DIM) + NORM_EPS) * gain

    q = head_norm(proj(C_Q, C_K), qg_ref[...])
    qb_ref[...] = (q * (LOG2E * HEAD_DIM ** -0.5)).astype(BF16)
    k = head_norm(proj(C_K, C_V), kg_ref[...])
    k_ref[...] = k
    kb_ref[...] = k.astype(BF16)


def _in_proj(x, prm, layer, tm, kv_acc):
    n = x.shape[0]
    depth = prm["w_in"].shape[0]
    row = lambda w: pl.BlockSpec((tm, w), lambda i: (i, 0))
    acc_row = pl.BlockSpec((None, tm, D_DIFF), lambda i: (layer, i, 0))
    out_shapes = [
        jax.ShapeDtypeStruct((n, SHIFT_MAIN), F32),
        jax.ShapeDtypeStruct((n, 2 * LORA), F32),
        jax.ShapeDtypeStruct((n, D_RWKV), BF16),
        jax.ShapeDtypeStruct((n, D_DIFF), BF16),
        jax.ShapeDtypeStruct((n, 2 * D_MODEL), BF16),
        jax.ShapeDtypeStruct((depth, n, D_DIFF), F32),
        jax.ShapeDtypeStruct((depth, n, D_DIFF), F32),
        jax.ShapeDtypeStruct((n, D_DIFF), BF16),
        jax.ShapeDtypeStruct((n, D_DIFF), BF16),
        jax.ShapeDtypeStruct((n, D_DIFF), BF16),
    ]
    params = [prm["norm_g"], prm["w_in"], prm["qg"], prm["kg"]]
    acc = [] if kv_acc is None else list(kv_acc)
    n_in = 1 + len(params) + 1
    return pl.pallas_call(
        _in_proj_kernel,
        grid=(n // tm,),
        in_specs=[row(D_MODEL)] + [_layer_spec(p, layer) for p in params]
                 + [_const_spec(prm["bd"].shape)] + [pl.BlockSpec(memory_space=pl.ANY)] * len(acc),
        out_specs=[acc_row if len(s.shape) == 3 else row(s.shape[1]) for s in out_shapes],
        out_shape=out_shapes,
        input_output_aliases={n_in + j: 5 + j for j in range(len(acc))},
        compiler_params=pltpu.CompilerParams(
            dimension_semantics=("parallel",), vmem_limit_bytes=VMEM_LIMIT),
        name="in_proj",
    )(x, *params, prm["bd"], *acc)


def _rwkv_kernel(zs_ref, zl_ref, gr_ref, spm_ref, spl_ref, s0_ref,
                 mum_ref, mul_ref, wl_ref, w0_ref, a0_ref, kk_ref, ka_ref, rk_ref,
                 lg_ref, lb_ref, bd_ref,
                 o_ref, so_ref,
                 cm_scr, cl_scr, s_scr, *, tb):
    t = pl.program_id(1)
    c = CHUNK
    n_chunks = tb // c

    @pl.when(t == 0)
    def _():
        cm_scr[...] = spm_ref[0]
        cl_scr[...] = spl_ref[0]
        s_scr[...] = s0_ref[0]

    row = lax.broadcasted_iota(jnp.int32, (tb, 1), 0)

    def shifted(z_ref, carry, mu_ref):
        z = z_ref[0]
        zp = jnp.where(row == 0, carry[...], pltpu.roll(z, 1, axis=0))
        carry[...] = z[tb - 1:tb, :]
        return z + (zp - z) * mu_ref[...]

    zm = shifted(zs_ref, cm_scr, mum_ref)
    zl = shifted(zl_ref, cl_scr, mul_ref)
    r_ = zm[:, :D_RWKV]
    k_ = zm[:, D_RWKV:2 * D_RWKV]
    v_ = zm[:, 2 * D_RWKV:]

    lane = lax.broadcasted_iota(jnp.int32, (tb, 2 * LORA), 1)
    lin = jnp.where(lane < LORA, jnp.tanh(zl), zl)
    lo = _dot(lin, wl_ref[...])
    wpre = w0_ref[...] + lo[:, :D_RWKV]
    apre = a0_ref[...] + lo[:, D_RWKV:]
    sp = jnp.maximum(-wpre, 0.0) + jnp.log(1.0 + jnp.exp(-jnp.abs(wpre)))
    lw = -jnp.exp(-sp - 0.5)
    a = _sigmoid(apre)

    bd = bd_ref[...]

    def head_sum(u):
        return _head_sums(u, bd, HEAD_SUM_TERMS)

    kkr = k_ * kk_ref[...]
    kk = kkr / jnp.maximum(jnp.sqrt(head_sum(kkr * kkr)), 1e-12)
    kf = k_ * (1.0 + (a - 1.0) * ka_ref[...])

    ri = lax.broadcasted_iota(jnp.int32, (tb, tb), 0)
    ci = lax.broadcasted_iota(jnp.int32, (tb, tb), 1)
    same = (ri >> 6) == (ci >> 6)
    tri = jnp.where(jnp.logical_and(same, ci <= ri), 1.0, 0.0).astype(BF16)
    cs = _dot_exact_lhs(tri, lw, 3)
    if n_chunks == 1:
        ctot = cs[c - 1:c, :]
    else:
        ctot = _dot_exact_lhs(jnp.where(same, 1.0, 0.0).astype(BF16), lw, 3)
    g = jnp.exp(cs)
    ginv = jnp.exp(-cs)
    gend = jnp.exp(ctot - cs)
    ka = kk * a
    a_t = -kk * jnp.exp(cs - lw)
    b_t = ka * ginv
    k_t = kf * ginv
    r_t = r_ * g
    b_h = ka * gend
    k_h = kf * gend

    lane_p = lax.broadcasted_iota(jnp.int32, (c, PAIR), 1)
    first = lane_p < HEAD_DIM

    def stack(u):
        return jnp.concatenate([jnp.where(first, u, 0.0), jnp.where(first, 0.0, u)], axis=0)

    r2 = lax.broadcasted_iota(jnp.int32, (2 * c, 2 * c), 0)
    c2 = lax.broadcasted_iota(jnp.int32, (2 * c, 2 * c), 1)
    strict = c2 < r2
    eye = jnp.where(c2 == r2, 1.0, 0.0)
    r4 =lax.broadcasted_iota(jnp.int32, (2 * c, 4 * c), 0)
    c4 = lax.broadcasted_iota(jnp.int32, (2 * c, 4 * c), 1)
    incl = jnp.where(c4 >= 2 * c, c4 - 2 * c, c4) <= r4

    def prepare(blocks):
        def gather(u):
            return [stack(u[ch * c:(ch + 1) * c, p * PAIR:(p + 1) * PAIR]).astype(BF16)
                    for ch, p in blocks]

        a_s, r_s, b_s, k_s, v_s = gather(a_t), gather(r_t), gather(b_t), gather(k_t), gather(v_)
        bkh = [jnp.concatenate([x, y], axis=0) for x, y in zip(gather(b_h), gather(k_h))]
        gmat = [_dot_nt(jnp.concatenate([x, y], axis=0), jnp.concatenate([z, w], axis=0))
                for x, y, z, w in zip(a_s, r_s, b_s, k_s)]
        a_ak = [jnp.where(strict, gm[:2 * c, 2 * c:], 0.0) for gm in gmat]
        a_r = [jnp.where(incl, gm[2 * c:, :], 0.0).astype(BF16) for gm in gmat]
        lp = [jnp.where(strict, gm[:2 * c, :2 * c], 0.0) for gm in gmat]
        inv = [eye + x for x in lp]
        for _ in range(5):
            lp = [_dot(x, x) for x in lp]
            inv = [y + _dot(y, x) for x, y in zip(lp, inv)]
        w_s = [_dot(y, x).astype(BF16) for x, y in zip(a_s, inv)]
        x_s = [_dot(x, y) for x, y in zip(a_ak, v_s)]
        x_s = [_dot(y, x) for x, y in zip(x_s, inv)]
        return w_s, x_s, a_r, r_s, v_s, bkh

    blocks = [(ch, p) for ch in range(n_chunks) for p in range(N_PAIRS)]
    groups = [prepare(blocks[i:i + PREP_GROUP]) for i in range(0, len(blocks), PREP_GROUP)]
    w_s, x_s, a_r, r_s, v_s, bkh = (sum(parts, []) for parts in zip(*groups))

    outs = []
    pairs = range(N_PAIRS)
    s = [s_scr[p] for p in pairs]
    for ch in range(n_chunks):
        glast = jnp.exp(ctot[ch * c:ch * c + 1, :])
        at = lambda lst: lst[ch * N_PAIRS:(ch + 1) * N_PAIRS]
        u_s = [_dot_nt(w, sp) + x for w, sp, x in zip(at(w_s), s, at(x_s))]
        uv = [jnp.concatenate([u.astype(BF16), v], axis=0) for u, v in zip(u_s, at(v_s))]
        rs = [_dot_nt(r, sp) for r, sp in zip(at(r_s), s)]
        o_s = [x + _dot(ar, u) for x, ar, u in zip(rs, at(a_r), uv)]
        upd = [_dot_tn(u, bk) for u, bk in zip(uv, at(bkh))]
        s = [s[p] * glast[:, p * PAIR:(p + 1) * PAIR] + upd[p] for p in pairs]
        outs.append(jnp.concatenate([x[:c] + x[c:] for x in o_s], axis=1))
    for p in pairs:
        s_scr[p] = s[p]
    o = jnp.concatenate(outs, axis=0)

    mu = head_sum(o) * (1.0 / HEAD_DIM)
    d = o - mu
    var = head_sum(d * d) * (1.0 / HEAD_DIM)
    bonus = head_sum(r_ * kf * rk_ref[...]) * v_
    yn = d * lax.rsqrt(var + GN_EPS) * lg_ref[...] + lb_ref[...] + bonus
    o_ref[0] = (yn * _silu(gr_ref[0].astype(F32))).astype(o_ref.dtype)

    @pl.when(t == pl.num_programs(1) - 1)
    def _():
        so_ref[0] = s_scr[...]


def _rwkv(zs, zl, gr, spm, spl, s0, prm, layer, tb):
    b, t, _ = zs.shape
    seq = lambda w: pl.BlockSpec((1, tb, w), lambda i, j: (i, j, 0))
    per_b = lambda shp: pl.BlockSpec((1,) + shp, lambda i, j: (i,) + (0,) * len(shp))
    params = [prm["mu_main"], prm["mu_lora"], prm["w_lora"], prm["w0"], prm["a0"], prm["k_k"],
              prm["k_a"], prm["r_k"], prm["lnx_g"], prm["lnx_b"]]
    return pl.pallas_call(
        functools.partial(_rwkv_kernel, tb=tb),
        grid=(b, t // tb),
        in_specs=[seq(SHIFT_MAIN), seq(2 * LORA), seq(D_RWKV),
                  per_b((1, SHIFT_MAIN)), per_b((1, 2 * LORA)), per_b((N_PAIRS, PAIR, PAIR))]
                 + [_layer_spec(p, layer) for p in params] + [_const_spec(prm["bd"].shape)],
        out_specs=[seq(D_RWKV), per_b((N_PAIRS, PAIR, PAIR))],
        out_shape=[jax.ShapeDtypeStruct((b, t, D_RWKV), F32),
                   jax.ShapeDtypeStruct((b, N_PAIRS, PAIR, PAIR), F32)],
        scratch_shapes=[pltpu.VMEM((1, SHIFT_MAIN), F32), pltpu.VMEM((1, 2 * LORA), F32),
                        pltpu.VMEM((N_PAIRS, PAIR, PAIR), F32)],
        compiler_params=pltpu.CompilerParams(
            dimension_semantics=("parallel", "arbitrary"), vmem_limit_bytes=VMEM_LIMIT),
        name="rwkv",
    )(zs, zl, gr, spm, spl, s0, *params, prm["bd"])


def _attn_kernel(scal_ref, q_ref, k_ref, v_ref, *refs, layer, bq, bkf, q_off, has_cache):
    if has_cache:
        kp_ref, vp_ref, *refs = refs
    else:
        kp_ref, vp_ref = k_ref, v_ref
    ga_ref, sg_ref, o_ref, acc_scr, diag_scr, *scr = refs
    hp = pl.program_id(1)
    qi = pl.program_id(2)
    heads = range(HEADS_PER_STEP)
    lanes = lambda hh: slice(hh * PAIR, (hh + 1) * PAIR)
    slope = [scal_ref[layer, HEADS_PER_STEP * hp + hh] * LOG2E for hh in heads]
    folded = bkf == bq
    if folded:
        kx_scr, qx_scr, st_a, st_b = scr
    else:
        rel_scr, = scr

    @pl.when(qi == 0)
    def _():
        i_d = lax.broadcasted_iota(jnp.int32, (bq, 2 * bq), 1)
        i_d = jnp.where(i_d >= bq, i_d - bq, i_d)
        j_d = lax.broadcasted_iota(jnp.int32, (bq, 2 * bq), 0)
        dist = jnp.abs(i_d - j_d).astype(F32)
        allowed = (j_d >> 6) <= (i_d >> 6)
        for hh in heads:
            diag_scr[hh] = jnp.where(allowed, -slope[hh] * dist, NEG_INF)
            if folded:
                def extra(rows, pos_first):
                    lane = lax.broadcasted_iota(jnp.int32, (rows, PAIR), 1)
                    pos = lax.broadcasted_iota(jnp.int32, (rows, PAIR), 0)
                    pos = jnp.where(pos >= bq, pos - bq, pos).astype(F32)
                    terms = _split_bf16(jnp.full((rows, PAIR), slope[hh], F32), BIAS_LANES)
                    term = terms[BIAS_LANES - 1].astype(F32)
                    for n in range(BIAS_LANES - 1):
                        here = jnp.logical_or(lane == n, lane == n + BIAS_LANES)
                        term = jnp.where(here, terms[n].astype(F32), term)
                    lo, hi = (pos, term) if pos_first else (term, -pos)
                    out = jnp.where(lane < BIAS_LANES, lo,
                                    jnp.where(lane < 2 * BIAS_LANES, hi, 0.0))
                    return out.astype(BF16)

                kx_scr[hh] = extra(bkf, True)
                qx_scr[hh] = extra(2 * bq, False)
            else:
                i_f =lax.broadcasted_iota(jnp.int32, (bkf, 2 * bq), 1)
                i_f = jnp.where(i_f >= bq, i_f - bq, i_f)
                j_f = lax.broadcasted_iota(jnp.int32, (bkf, 2 * bq), 0)
                rel_scr[hh] = (j_f - i_f).astype(F32) * slope[hh]

    first = lax.broadcasted_iota(jnp.int32, (bq, PAIR), 1) < HEAD_DIM
    qs = []
    for hh in heads:
        q = q_ref[0, :, lanes(hh)]
        zero = jnp.zeros_like(q)
        qs.append(jnp.concatenate([jnp.where(first, q, zero), jnp.where(first, zero, q)], axis=0))
    acc_scr[...] = jnp.zeros(acc_scr.shape, F32)
    q0 = q_off + qi * bq
    nt = (((1,), (1,)), ((), ()))

    def tile(ref, k0, size, hh):
        return ref[0, pl.ds(k0, size), lanes(hh)].astype(BF16)

    def scores(ref, k0, size, hh):
        return lax.dot_general(tile(ref, k0, size, hh), qs[hh], nt, preferred_element_type=F32)

    def absorb(sts, ref, k0, size, offs, carry):
        new, upd = [], []
        for hh in heads:
            m, l, st = carry[2 * hh], carry[2 * hh + 1], sts[hh]
            m_new = jnp.maximum(m, jnp.max(st, axis=0, keepdims=True) + offs[hh])
            p = jnp.exp2(st - (m_new - offs[hh]))
            alpha = jnp.exp2(m - m_new)
            new += [m_new, alpha * l + jnp.sum(p, axis=0, keepdims=True)]
            pv = lax.dot_general(tile(ref, k0, size, hh), p.astype(BF16),
                                 (((0,), (0,)), ((), ())), preferred_element_type=F32)
            upd.append((alpha, pv))
        for hh in heads:
            acc_scr[hh] = upd[hh][0] * acc_scr[hh] + upd[hh][1]
        return tuple(new)

    def rel_offs(k0):
        return [-slope[hh] * (q0 - k0).astype(F32) for hh in heads]

    carry = (jnp.full((1, 2 * bq), NEG_INF, F32), jnp.zeros((1, 2 * bq), F32)) * HEADS_PER_STEP
    n_full = q0 // bkf
    kd = pl.multiple_of(qi * bq + (0 if has_cache else q_off), bq)
    no_offs = [0.0] * HEADS_PER_STEP
    if folded:
        qa =[jnp.concatenate([qs[hh], qx_scr[hh]], axis=1) for hh in heads]

        def scores_rel(k0, hh):
            ka = jnp.concatenate([tile(kp_ref, k0, bkf, hh), kx_scr[hh]], axis=1)
            return lax.dot_general(ka, qa[hh], nt, preferred_element_type=F32)

        for hh in heads:
            st_a[hh] = scores_rel(0, hh)

        def body(i, carry):
            k0 = pl.multiple_of(2 * i * bkf, bkf)
            k1 = pl.multiple_of(k0 + bkf, bkf)
            k2 = pl.multiple_of(k1 + bkf, bkf)
            for hh in heads:
                st_b[hh] = scores_rel(k1, hh)
            carry = absorb([st_a[hh] for hh in heads], vp_ref, k0, bkf, rel_offs(k0), carry)
            for hh in heads:
                st_a[hh] = scores_rel(k2, hh)
            return absorb([st_b[hh] for hh in heads], vp_ref, k1, bkf, rel_offs(k1), carry)

        carry = lax.fori_loop(0, n_full // 2, body, carry)
        sd = [scores(k_ref, kd, bq, hh) + diag_scr[hh] for hh in heads]
        odd = (n_full & 1) == 1
        ko = pl.multiple_of(jnp.maximum(n_full - 1, 0) * bkf, bkf)
        pad = jnp.where(odd, 0.0, NO_KEYS)
        offs = [jnp.where(odd, o, 0.0) for o in rel_offs(ko)]
        carry = absorb([st_a[hh] + pad for hh in heads], vp_ref, ko, bkf, offs, carry)
    else:
        def body(j, carry):
            k0 = pl.multiple_of(j * bkf, bkf)
            return absorb([scores(kp_ref, k0, bkf, hh) + rel_scr[hh] for hh in heads], vp_ref, k0,
                          bkf, rel_offs(k0), carry)

        carry = lax.fori_loop(0, n_full, body, carry)
        sd = [scores(k_ref, kd, bq, hh) + diag_scr[hh] for hh in heads]
    carry = absorb(sd, v_ref, kd, bq, no_offs, carry)

    lam = scal_ref[layer, N_DIFF_HEADS]
    post = scal_ref[layer, N_DIFF_HEADS + 1]
    outs = []
    for hh in heads:
        on = acc_scr[hh] / carry[2 * hh + 1]
        ot = on[:, :bq] - lam * on[:, bq:]
        ot = ot * lax.rsqrt(jnp.mean(ot * ot, axis=0, keepdims=True) + SUBLN_EPS)
        outs.append((ot * sg_ref[...] * post).T)
    o = jnp.concatenate(outs, axis=1)
    o_ref[0] = (o * _silu(ga_ref[0].astype(F32))).astype(o_ref.dtype)


def _attn(qb, kb, vb, ga, prm, layer, bq, bkf, cache):
    b, tq, _ = qb.shape
    q_off = 0 if cache is None else cache[0].shape[2]
    assert q_off % bkf == 0 and (bq % bkf == 0 or tq == bq)
    assert q_off % CHUNK == 0 and bq % CHUNK == 0
    assert bq <= 256
    hs = HEADS_PER_STEP
    q_spec = pl.BlockSpec((1, bq, hs * PAIR), lambda i, h, qi: (i, qi, h))
    kv_spec = lambda tk: pl.BlockSpec((1, tk, hs * PAIR), lambda i, h, qi: (i, 0, h))
    past = [] if cache is None else list(cache)
    past_spec = pl.BlockSpec((None, 1, q_off, hs * PAIR), lambda i, h, qi: (layer, i, 0, h))
    if bkf == bq:
        extra = [pltpu.VMEM((hs, bkf, PAIR), BF16), pltpu.VMEM((hs, 2 * bq, PAIR), BF16),
                 pltpu.VMEM((hs, bkf, 2 * bq), F32), pltpu.VMEM((hs, bkf, 2 * bq), F32)]
    else:
        extra = [pltpu.VMEM((hs, bkf, 2 * bq), F32)]
    kern = functools.partial(_attn_kernel, layer=layer, bq=bq, bkf=bkf, q_off=q_off,
                             has_cache=cache is not None)
    return pl.pallas_call(
        kern,
        grid=(b, N_DIFF_HEADS // hs, tq // bq),
        in_specs=[pl.BlockSpec(memory_space=pltpu.SMEM), q_spec, kv_spec(tq), kv_spec(tq)]
                 + [past_spec] * len(past)
                 + [q_spec, _layer_spec(prm["subln_col"], layer)],
        out_specs=q_spec,
        out_shape=jax.ShapeDtypeStruct((b, tq, D_DIFF), F32),
        scratch_shapes=[pltpu.VMEM((hs, PAIR, 2 * bq), F32), pltpu.VMEM((hs, bq, 2 * bq), F32)]
                       + extra,
        compiler_params=pltpu.CompilerParams(
            dimension_semantics=("arbitrary", "arbitrary", "arbitrary"),
            vmem_limit_bytes=VMEM_LIMIT),
        name="attn",
    )(prm["scal"], qb, kb, vb, *past, ga, prm["subln_col"])


def _merge_kernel(x_ref, or_ref, oa_ref, mg_ref, p_ref, wr_ref, wa_ref, wo_ref, pw_ref, pg_ref,
                  png_ref, y_ref):
    mg = mg_ref[...].astype(F32)
    u = (_sigmoid(mg[:, :D_MODEL]) * _dot(or_ref[...], wr_ref[...])
         + _sigmoid(mg[:, D_MODEL:]) * _dot(oa_ref[...], wa_ref[...]))
    x = x_ref[...] + _dot(u, wo_ref[...])
    e = _dot(p_ref[...], pw_ref[...])
    ms = jnp.mean(x * x, axis=-1, keepdims=True)
    hn = x * lax.rsqrt(ms + NORM_EPS) * png_ref[...]
    y_ref[...] = x + e * _sigmoid(_dot(hn, pg_ref[...]))


def _merge(x, o_r, o_a, mg, p_all, prm, layer, tm):
    n = x.shape[0]
    row = lambda w: pl.BlockSpec((tm, w), lambda i: (i, 0))
    p_spec = pl.BlockSpec((None, tm, PLE_DIM), lambda i: (layer, i, 0))
    ws = [prm["w_br_r"], prm["w_br_a"], prm["w_out"], prm["ple_w"], prm["ple_gate_w"],
          prm["ple_norm_g"]]
    return pl.pallas_call(
        _merge_kernel,
        grid=(n // tm,),
        in_specs=[row(D_MODEL), row(D_RWKV), row(D_DIFF), row(2 * D_MODEL), p_spec]
                 + [_layer_spec(w, layer) for w in ws],
        out_specs=row(D_MODEL),
        out_shape=jax.ShapeDtypeStruct((n, D_MODEL), F32),
        compiler_params=pltpu.CompilerParams(
            dimension_semantics=("parallel",), vmem_limit_bytes=VMEM_LIMIT),
        name="merge",
    )(x, o_r, o_a, mg, p_all, *ws)


def _prepare(norm_g, w_in, shift_mu, decay_w0, decay_w2, iclr_a0, iclr_a2, k_k, k_a, r_k, lnx_g,
             lnx_b, q_norm_g, k_norm_g, lambda_q1, lambda_k1, lambda_q2, lambda_k2, subln_g,
             w_br_r, w_br_a, w_out, ple_w, ple_gate_w, ple_norm_g):
    depth = w_in.shape[0]
    vec = lambda u: u.reshape(depth, 1, -1)
    zeros = jnp.zeros((depth, LORA, D_RWKV), F32)
    w_lora = jnp.concatenate([jnp.concatenate([decay_w2, zeros], axis=2),
                              jnp.concatenate([zeros, iclr_a2], axis=2)], axis=1).astype(BF16)
    lam_init = jnp.asarray([0.8 - 0.6 * math.exp(-0.3 * i) for i in range(depth)], F32)
    lam = (jnp.exp(jnp.sum(lambda_q1 * lambda_k1, axis=-1))
           - jnp.exp(jnp.sum(lambda_q2 * lambda_k2, axis=-1)) + lam_init)
    slopes = jnp.asarray([2.0 ** (-8.0 * (h + 1) / N_DIFF_HEADS) for h in range(N_DIFF_HEADS)], F32)
    scal = jnp.concatenate([jnp.broadcast_to(slopes, (depth, N_DIFF_HEADS)), lam[:, None],
                            (1.0 - lam_init)[:, None]], axis=1).astype(F32)
    hid = jnp.arange(HEAD_SUM_SLAB) // HEAD_DIM
    return dict(
        norm_g=vec(norm_g), w_in=w_in.astype(BF16),
        qg=vec(jnp.tile(q_norm_g, (1, 2 * N_DIFF_HEADS))), kg=vec(jnp.tile(k_norm_g, (1, 2 * N_DIFF_HEADS))),
        mu_main=vec(shift_mu[:, :SHIFT_MAIN]), mu_lora=vec(shift_mu[:, SHIFT_MAIN:]), w_lora=w_lora,
        w0=vec(decay_w0), a0=vec(iclr_a0), k_k=vec(k_k), k_a=vec(k_a), r_k=vec(r_k),
        lnx_g=vec(lnx_g), lnx_b=vec(lnx_b), scal=scal, subln_col=subln_g.reshape(depth, PAIR, 1),
        w_br_r=w_br_r.astype(BF16), w_br_a=w_br_a.astype(BF16), w_out=w_out.astype(BF16),
        ple_w=ple_w.astype(BF16), ple_gate_w=ple_gate_w.astype(BF16), ple_norm_g=vec(ple_norm_g),
        bd=(hid[:, None] == hid[None, :]).astype(BF16))


def _pair_state(s):
    b = s.shape[0]
    s = s.reshape(b, N_PAIRS, 2, HEAD_DIM, HEAD_DIM)
    z = jnp.zeros_like(s[:, :, 0])
    top = jnp.concatenate([s[:, :, 0], z], axis=-1)
    bot = jnp.concatenate([z, s[:, :, 1]], axis=-1)
    return jnp.concatenate([top, bot], axis=-2)


def _unpair_state(sp):
    b = sp.shape[0]
    h0 = sp[:, :, :HEAD_DIM, :HEAD_DIM]
    h1 = sp[:, :, HEAD_DIM:, HEAD_DIM:]
    return jnp.stack([h0, h1], axis=2).reshape(b, 2 * N_PAIRS, HEAD_DIM, HEAD_DIM)


def _group_layer(x, p_all, prm, layer, shift_prev, wkv_prev, cache, kv_acc):
    b, t, _ = x.shape
    n = b * t
    tl = _tiles(t)
    zs, zl, gr, ga, mg, k_acc, v_acc, qb, kb, vb = _in_proj(x.reshape(n, D_MODEL), prm, layer,
                                                            tl["tm"], kv_acc)
    seq = lambda u: u.reshape(b, t, u.shape[-1])
    zs, zl = seq(zs), seq(zl)
    o_r, s_new = _rwkv(zs, zl, seq(gr), shift_prev[:, None, :SHIFT_MAIN],
                       shift_prev[:, None, SHIFT_MAIN:], _pair_state(wkv_prev), prm, layer,
                       tl["tb"])
    o_a = _attn(seq(qb), seq(kb), seq(vb), seq(ga), prm, layer, tl["bq"], tl["bkf"], cache)
    y = _merge(x.reshape(n, D_MODEL), o_r.reshape(n, D_RWKV), o_a.reshape(n, D_DIFF), mg,
               p_all.reshape(p_all.shape[0], n, PLE_DIM), prm, layer, tl["tm"])
    shift_new = jnp.concatenate([zs[:, -1], zl[:, -1]], axis=-1)
    return y.reshape(b, t, D_MODEL), (k_acc, v_acc), _unpair_state(s_new), shift_new


def kernel(x_prompt, x_sample, p_prompt, p_sample, cache_k, cache_v, state_wkv, state_shift, norm_g, w_in, shift_mu, decay_w0, decay_w2, iclr_a0, iclr_a2, k_k, k_a, r_k, lnx_g, lnx_b, q_norm_g, k_norm_g, lambda_q1, lambda_k1, lambda_q2, lambda_k2, subln_g, w_br_r, w_br_a, w_out, ple_w, ple_gate_w, ple_norm_g):
    depth = w_in.shape[0]
    b_p = x_prompt.shape[0]
    prm = _prepare(norm_g, w_in, shift_mu, decay_w0, decay_w2, iclr_a0, iclr_a2, k_k, k_a, r_k,
                   lnx_g, lnx_b, q_norm_g, k_norm_g, lambda_q1, lambda_k1, lambda_q2, lambda_k2,
                   subln_g, w_br_r, w_br_a, w_out, ple_w, ple_gate_w, ple_norm_g)
    zero_shift = jnp.zeros((b_p, SHIFT_MAIN + 2 * LORA), F32)
    zero_wkv = jnp.zeros((b_p, 2 * N_PAIRS, HEAD_DIM, HEAD_DIM), F32)
    cache = tuple(u.reshape(u.shape[:3] + (D_DIFF,)) for u in (cache_k, cache_v))
    y_p, y_s = x_prompt, x_sample
    outs_p, outs_s = [], []
    kv_p = kv_s = None
    for i in range(depth):
        y_p, kv_p, *rest_p = _group_layer(y_p, p_prompt, prm, i, zero_shift, zero_wkv, None, kv_p)
        y_s, kv_s, *rest_s = _group_layer(y_s, p_sample, prm, i, state_shift[i], state_wkv[i],
                                          cache, kv_s)
        outs_p.append(rest_p)
        outs_s.append(rest_s)
    stack = lambda outs, j: jnp.stack([o[j] for o in outs])

    def kv_out(kv, like):
        lead = (depth,) + like.shape[:2]
        return (kv[0].reshape(lead + (N_DIFF_HEADS, 2, HEAD_DIM)),
                kv[1].reshape(lead + (N_DIFF_HEADS, 2 * HEAD_DIM)))

    return (y_p, y_s, *kv_out(kv_p, x_prompt), stack(outs_p, 0), stack(outs_p, 1),
            *kv_out(kv_s, x_sample), stack(outs_s, 0), stack(outs_s, 1))
```
